```python
import jax, jax.numpy as jnp
from jax import lax
import numpy as np

D_MODEL = 2048
BATCH = 1
SEQ = 8192
DEPTH = 4

N_MIXERS = 4
D_FF = 4 * D_MODEL
CONV_WIDTH = 4
CHUNK = 64
NORM_EPS = 1e-6

N_GDN_LAYERS = (DEPTH + 3) // 4
N_MLSTM_LAYERS = (DEPTH + 2) // 4
N_RET_LAYERS = (DEPTH + 1) // 4
N_LRU_LAYERS = DEPTH // 4

GDN_HEADS = 16
GDN_DK = D_MODEL // GDN_HEADS
GDN_DV = D_MODEL // GDN_HEADS
GDN_PROJ = 2 * GDN_HEADS * GDN_DK + 2 * GDN_HEADS * GDN_DV + 2 * GDN_HEADS
MLSTM_HEADS = 8
MLSTM_DQK = D_MODEL // 2 // MLSTM_HEADS
MLSTM_DV = D_MODEL // MLSTM_HEADS
MLSTM_PROJ = 2 * MLSTM_HEADS * MLSTM_DQK + 2 * MLSTM_HEADS * MLSTM_DV + 2 * MLSTM_HEADS
GATE_SOFTCAP = 15.0
RET_HEADS = 8
RET_DK = D_MODEL // RET_HEADS
RET_DV = 2 * D_MODEL // RET_HEADS
RET_PROJ = 2 * RET_HEADS * RET_DK + 2 * RET_HEADS * RET_DV
ROPE_BASE = 10000.0
LRU_WIDTH = D_MODEL
LRU_BLOCKS = 16
LRU_BLOCK = LRU_WIDTH // LRU_BLOCKS
LRU_C = 8.0

kernel_name = 'hybrid_interleaved_gdn_mlstm_retnet_rglru'


def rmsnorm(x, g):
    xf = x.astype(jnp.float32)
    y = xf * lax.rsqrt(jnp.mean(xf * xf, axis=-1, keepdims=True) + NORM_EPS)
    return (y * g.astype(jnp.float32)).astype(x.dtype)


def l2norm(t):
    return t * lax.rsqrt(jnp.sum(t * t, axis=-1, keepdims=True) + NORM_EPS)


def softcap(t):
    return GATE_SOFTCAP * jnp.tanh(t / GATE_SOFTCAP)


def causal_conv(x, w):
    s = x.shape[1]
    xp = jnp.pad(x, ((0, 0), (CONV_WIDTH - 1, 0), (0, 0)))
    return sum(w[k] * xp[:, k:k + s] for k in range(CONV_WIDTH))


def to_chunks(t):
    b, s, h = t.shape[:3]
    t = t.reshape(b, s // CHUNK, CHUNK, h, *t.shape[3:])
    return jnp.moveaxis(t, (1, 3), (0, 2))


def from_chunks(t):
    n, b, h, l, d = t.shape
    return jnp.transpose(t, (1, 0, 3, 2, 4)).reshape(b, n * l, h, d)


def causal_masks():
    incl = jnp.tril(jnp.ones((CHUNK, CHUNK), dtype=bool))
    strict = jnp.tril(jnp.ones((CHUNK, CHUNK), dtype=bool), -1)
    return incl, strict


def rope(t, positions):
    d = t.shape[-1]
    freqs = ROPE_BASE ** (-jnp.arange(0, d, 2, dtype=jnp.float32) / d)
    ang = positions.astype(jnp.float32)[..., None, None] * freqs
    cos, sin = jnp.cos(ang), jnp.sin(ang)
    t1, t2 = jnp.split(t, 2, axis=-1)
    return jnp.concatenate([t1 * cos - t2 * sin, t1 * sin + t2 * cos], axis=-1)


def gated_deltanet(x, w_in, conv_w, a_log, dt_bias, norm_g, w_out):
    f32 = jnp.float32
    b_, s_, _ = x.shape
    h, dk, dv = GDN_HEADS, GDN_DK, GDN_DV
    proj = x @ w_in
    qkv, z, b_pre, a_pre = jnp.split(proj, [2 * h * dk + h * dv, 2 * h * dk + 2 * h * dv, 2 * h * dk + 2 * h * dv + h], axis=-1)
    qkv = jax.nn.silu(causal_conv(qkv, conv_w)).astype(f32)
    q, k, v = jnp.split(qkv, [h * dk, 2 * h * dk], axis=-1)
    q = l2norm(q.reshape(b_, s_, h, dk)) * dk ** -0.5
    k = l2norm(k.reshape(b_, s_, h, dk))
    v = v.reshape(b_, s_, h, dv)
    beta = jax.nn.sigmoid(b_pre.astype(f32))
    g = -jnp.exp(a_log.astype(f32)) * jax.nn.softplus(a_pre.astype(f32) + dt_bias.astype(f32))
    q, k, v, beta, g = map(to_chunks, (q, k, v, beta, g))
    gc = jnp.cumsum(g, axis=-1)
    incl, strict = causal_masks()
    decay = jnp.exp(jnp.where(incl, gc[..., :, None] - gc[..., None, :], -jnp.inf))
    a_mat = jnp.where(strict, beta[..., :, None] * jnp.einsum('nbhid,nbhjd->nbhij', k, k) * decay, 0.0)
    rhs = jnp.concatenate([v * beta[..., None], k * (beta * jnp.exp(gc))[..., None]], axis=-1)
    sol = lax.linalg.triangular_solve(a_mat + jnp.eye(CHUNK, dtype=f32), rhs, left_side=True, lower=True, unit_diagonal=True)
    u, w = sol[..., :dv], sol[..., dv:]
    attn = jnp.einsum('nbhid,nbhjd->nbhij', q, k) * decay

    def step(state, inp):
        q_c, k_c, u_c, w_c, gc_c, attn_c = inp
        v_new = u_c - jnp.einsum('bhlk,bhkv->bhlv', w_c, state)
        o = (jnp.einsum('bhlk,bhkv->bhlv', q_c * jnp.exp(gc_c)[..., None], state)
             + jnp.einsum('bhij,bhjv->bhiv', attn_c, v_new))
        g_last = gc_c[..., -1]
        state = (state * jnp.exp(g_last)[..., None, None]
                 + jnp.einsum('bhlk,bhlv->bhkv', k_c * jnp.exp(g_last[..., None] - gc_c)[..., None], v_new))
        return state, o

    s0 = jnp.zeros((b_, h, dk, dv), f32)
    _, o = lax.scan(step, s0, (q, k, u, w, gc, attn))
    o = from_chunks(o)
    o = rmsnorm(o, norm_g) * jax.nn.silu(z.reshape(b_, s_, h, dv).astype(f32))
    return o.reshape(b_, s_, h * dv).astype(x.dtype) @ w_out


def mlstm(x, w_in, b_i, b_f, norm_g, w_out):
    f32 = jnp.float32
    b_, s_, _ = x.shape
    h, dqk, dv = MLSTM_HEADS, MLSTM_DQK, MLSTM_DV
    proj = (x @ w_in).astype(f32)
    q, k, v, o_pre, i_pre, f_pre = jnp.split(
        proj, [h * dqk, 2 * h * dqk, 2 * h * dqk + h * dv, 2 * h * dqk + 2 * h * dv, 2 * h * dqk + 2 * h * dv + h], axis=-1)
    q = q.reshape(b_, s_, h, dqk) * dqk ** -0.5
    k = k.reshape(b_, s_, h, dqk)
    v = v.reshape(b_, s_, h, dv)
    i_log = softcap(i_pre + b_i.astype(f32))
    f_log = jax.nn.log_sigmoid(softcap(f_pre + b_f.astype(f32)))
    q, k, v, i_log, f_log = map(to_chunks, (q, k, v, i_log, f_log))
    incl, _ = causal_masks()
    bc = jnp.cumsum(f_log, axis=-1)
    d_log = jnp.where(incl, bc[..., :, None] - bc[..., None, :] + i_log[..., None, :], -jnp.inf)
    d_max = jnp.max(d_log, axis=-1)
    qk = jnp.einsum('nbhid,nbhjd->nbhij', q, k)
    end_log = bc[..., -1:] - bc + i_log
    end_max = jnp.max(end_log, axis=-1)

    def step(carry, inp):
        c_mem, n_mem, m = carry
        q_c, k_c, v_c, b_c, d_c, dmax_c, qk_c, end_c, endmax_c = inp
        inter = b_c + m[..., None]
        m_t = jnp.maximum(inter, dmax_c)
        s = qk_c * jnp.exp(d_c - m_t[..., None])
        w_inter = jnp.exp(inter - m_t)
        num = (w_inter[..., None] * jnp.einsum('bhlk,bhkv->bhlv', q_c, c_mem)
               + jnp.einsum('bhij,bhjv->bhiv', s, v_c))
        den = w_inter * jnp.einsum('bhlk,bhk->bhl', q_c, n_mem) + jnp.sum(s, axis=-1)
        hid = num / jnp.maximum(jnp.abs(den), jnp.exp(-m_t))[..., None]
        b_last = b_c[..., -1]
        m_new = jnp.maximum(b_last + m, endmax_c)
        decay_state = jnp.exp(b_last + m - m_new)
        k_w = k_c * jnp.exp(end_c - m_new[..., None])[..., None]
        c_mem = decay_state[..., None, None] * c_mem + jnp.einsum('bhlk,bhlv->bhkv', k_w, v_c)
        n_mem = decay_state[..., None] * n_mem + jnp.sum(k_w, axis=-2)
        return (c_mem, n_mem, m_new), hid

    carry0 = (jnp.zeros((b_, h, dqk, dv), f32), jnp.zeros((b_, h, dqk), f32), jnp.zeros((b_, h), f32))
    _, hid = lax.scan(step, carry0, (q, k, v, bc, d_log, d_max, qk, end_log, end_max))
    hid = rmsnorm(from_chunks(hid), norm_g.reshape(h, dv))
    hid = hid * jax.nn.sigmoid(o_pre.reshape(b_, s_, h, dv))
    return hid.reshape(b_, s_, h * dv).astype(x.dtype) @ w_out


def retention(x, positions, w_in, norm_g, w_out):
    f32 = jnp.float32
    b_, s_, _ = x.shape
    h, dk, dv = RET_HEADS, RET_DK, RET_DV
    proj = (x @ w_in).astype(f32)
    q, k, v, gate = jnp.split(proj, [h * dk, 2 * h * dk, 2 * h * dk + h * dv], axis=-1)
    q = rope(q.reshape(b_, s_, h, dk), positions)
    k = rope(k.reshape(b_, s_, h, dk), positions) * dk ** -0.5
    v = v.reshape(b_, s_, h, dv)
    log_gamma = jnp.log1p(-jnp.exp2(-5.0 - jnp.arange(h, dtype=f32)))
    idx = jnp.arange(CHUNK, dtype=f32)
    incl, _ = causal_masks()
    d_mat = jnp.where(incl, jnp.exp((idx[:, None] - idx[None, :]) * log_gamma[:, None, None]), 0.0)
    q, k, v = map(to_chunks, (q, k, v))
    intra = jnp.einsum('nbhij,nbhjv->nbhiv', jnp.einsum('nbhid,nbhjd->nbhij', q, k) * d_mat, v)
    xi = jnp.exp((idx + 1.0) * log_gamma[:, None])
    zeta = jnp.exp((CHUNK - 1.0 - idx) * log_gamma[:, None])
    chunk_decay = jnp.exp(CHUNK * log_gamma)

    def step(r, inp):
        q_c, k_c, v_c, intra_c = inp
        o = intra_c + jnp.einsum('bhlk,bhkv->bhlv', q_c, r) * xi[..., None]
        r = r * chunk_decay[:, None, None] + jnp.einsum('bhlk,bhlv->bhkv', k_c * zeta[..., None], v_c)
        return r, o

    _, o = lax.scan(step, jnp.zeros((b_, h, dk, dv), f32), (q, k, v, intra))
    o = rmsnorm(from_chunks(o), norm_g.reshape(h, dv))
    o = o * jax.nn.silu(gate.reshape(b_, s_, h, dv))
    return o.reshape(b_, s_, h * dv).astype(x.dtype) @ w_out


def rglru_block(x, w_in, conv_w, conv_b, w_r, b_r, w_i, b_i, lam, w_out):
    f32 = jnp.float32
    b_, s_, _ = x.shape
    proj = x @ w_in
    xb, gate = jnp.split(proj, 2, axis=-1)
    xb = (causal_conv(xb, conv_w) + conv_b).astype(f32)
    xblk = xb.reshape(b_, s_, LRU_BLOCKS, LRU_BLOCK)
    r = jax.nn.sigmoid(jnp.einsum('bsnc,ncd->bsnd', xblk, w_r.astype(f32)).reshape(b_, s_, LRU_WIDTH) + b_r)
    i = jax.nn.sigmoid(jnp.einsum('bsnc,ncd->bsnd', xblk, w_i.astype(f32)).reshape(b_, s_, LRU_WIDTH) + b_i)
    log_a = -LRU_C * r * jax.nn.softplus(-lam.astype(f32))
    a = jnp.exp(log_a)
    u = jnp.sqrt(-jnp.expm1(2.0 * log_a)) * (i * xb)

    def combine(left, right):
        a_l, u_l = left
        a_r, u_r = right
        return a_l * a_r, a_r * u_l + u_r

    _, hs = lax.associative_scan(combine, (a, u), axis=1)
    y = hs * jax.nn.gelu(gate.astype(f32))
    return y.astype(x.dtype) @ w_out


def squared_relu_mlp(x, w_up, w_down):
    return jnp.square(jax.nn.relu(x @ w_up)) @ w_down


def setup_inputs(seed: int = 0) -> dict:
    key = jax.random.key(seed)
    ks = iter(jax.random.split(key, 32))
    f32 = jnp.float32

    def nrm(shape, fan_in):
        return jax.random.normal(next(ks), shape, f32) * fan_in ** -0.5

    def gain(shape):
        return 1.0 + 0.05 * jax.random.normal(next(ks), shape, f32)

    def small(shape):
        return 0.01 * jax.random.normal(next(ks), shape, f32)

    x = jax.random.normal(next(ks), (BATCH, SEQ, D_MODEL), f32)
    positions = jnp.broadcast_to(jnp.arange(SEQ, dtype=jnp.int32), (BATCH, SEQ))
    norm_g = gain((DEPTH, 4, D_MODEL))
    mlp_w_up = nrm((DEPTH, D_MODEL, D_FF), D_MODEL)
    mlp_w_down = nrm((DEPTH, D_FF, D_MODEL), D_FF)

    gdn_w_in = nrm((N_GDN_LAYERS, D_MODEL, GDN_PROJ), D_MODEL)
    gdn_conv_w = nrm((N_GDN_LAYERS, CONV_WIDTH, 2 * GDN_HEADS * GDN_DK + GDN_HEADS * GDN_DV), CONV_WIDTH)
    gdn_a_log = jnp.log(jax.random.uniform(next(ks), (N_GDN_LAYERS, GDN_HEADS), f32, 1.0, 16.0))
    dt = jnp.exp(jax.random.uniform(next(ks), (N_GDN_LAYERS, GDN_HEADS), f32, math_log(1e-3), math_log(1e-1)))
    gdn_dt_bias = dt + jnp.log(-jnp.expm1(-dt))
    gdn_norm_g = gain((N_GDN_LAYERS, GDN_DV))
    gdn_w_out = nrm((N_GDN_LAYERS, GDN_HEADS * GDN_DV, D_MODEL), GDN_HEADS * GDN_DV)

    mlstm_w_in = nrm((N_MLSTM_LAYERS, D_MODEL, MLSTM_PROJ), D_MODEL)
    mlstm_b_i = 0.1 * jax.random.normal(next(ks), (N_MLSTM_LAYERS, MLSTM_HEADS), f32)
    mlstm_b_f = 3.0 + 3.0 * jax.random.uniform(next(ks), (N_MLSTM_LAYERS, MLSTM_HEADS), f32)
    mlstm_norm_g = gain((N_MLSTM_LAYERS, MLSTM_HEADS * MLSTM_DV))
    mlstm_w_out = nrm((N_MLSTM_LAYERS, MLSTM_HEADS * MLSTM_DV, D_MODEL), MLSTM_HEADS * MLSTM_DV)

    ret_w_in = nrm((N_RET_LAYERS, D_MODEL, RET_PROJ), D_MODEL)
    ret_norm_g = gain((N_RET_LAYERS, RET_HEADS * RET_DV))
    ret_w_out = nrm((N_RET_LAYERS, RET_HEADS * RET_DV, D_MODEL), RET_HEADS * RET_DV)

    lru_w_in = nrm((N_LRU_LAYERS, D_MODEL, 2 * LRU_WIDTH), D_MODEL)
    lru_conv_w = nrm((N_LRU_LAYERS, CONV_WIDTH, LRU_WIDTH), CONV_WIDTH)
    lru_conv_b = small((N_LRU_LAYERS, LRU_WIDTH))
    lru_w_r = nrm((N_LRU_LAYERS, LRU_BLOCKS, LRU_BLOCK, LRU_BLOCK), LRU_BLOCK)
    lru_b_r = small((N_LRU_LAYERS, LRU_WIDTH))
    lru_w_i = nrm((N_LRU_LAYERS, LRU_BLOCKS, LRU_BLOCK, LRU_BLOCK), LRU_BLOCK)
    lru_b_i = small((N_LRU_LAYERS, LRU_WIDTH))
    a_c = jax.random.uniform(next(ks), (N_LRU_LAYERS, LRU_WIDTH), f32, 0.9, 0.999)
    a0 = a_c ** (1.0 / LRU_C)
    lru_lambda = jnp.log(a0) - jnp.log1p(-a0)
    lru_w_out = nrm((N_LRU_LAYERS, LRU_WIDTH, D_MODEL), LRU_WIDTH)

    return {'x': x, 'positions': positions, 'norm_g': norm_g, 'mlp_w_up': mlp_w_up, 'mlp_w_down': mlp_w_down,
            'gdn_w_in': gdn_w_in, 'gdn_conv_w': gdn_conv_w, 'gdn_a_log': gdn_a_log, 'gdn_dt_bias': gdn_dt_bias,
            'gdn_norm_g': gdn_norm_g, 'gdn_w_out': gdn_w_out,
            'mlstm_w_in': mlstm_w_in, 'mlstm_b_i': mlstm_b_i, 'mlstm_b_f': mlstm_b_f, 'mlstm_norm_g': mlstm_norm_g,
            'mlstm_w_out': mlstm_w_out,
            'ret_w_in': ret_w_in, 'ret_norm_g': ret_norm_g, 'ret_w_out': ret_w_out,
            'lru_w_in': lru_w_in, 'lru_conv_w': lru_conv_w, 'lru_conv_b': lru_conv_b, 'lru_w_r': lru_w_r,
            'lru_b_r': lru_b_r, 'lru_w_i': lru_w_i, 'lru_b_i': lru_b_i, 'lru_lambda': lru_lambda, 'lru_w_out': lru_w_out}


def math_log(v):
    return float(np.log(v))


def reference(x, positions, norm_g, mlp_w_up, mlp_w_down,
              gdn_w_in, gdn_conv_w, gdn_a_log, gdn_dt_bias, gdn_norm_g, gdn_w_out,
              mlstm_w_in, mlstm_b_i, mlstm_b_f, mlstm_norm_g, mlstm_w_out,
              ret_w_in, ret_norm_g, ret_w_out,
              lru_w_in, lru_conv_w, lru_conv_b, lru_w_r, lru_b_r, lru_w_i, lru_b_i, lru_lambda, lru_w_out):
    for layer in range(DEPTH):
        kind, j = layer % N_MIXERS, layer // N_MIXERS
        h = rmsnorm(x, norm_g[layer, 0])
        if kind == 0:
            h = gated_deltanet(h, gdn_w_in[j], gdn_conv_w[j], gdn_a_log[j], gdn_dt_bias[j], gdn_norm_g[j], gdn_w_out[j])
        elif kind == 1:
            h = mlstm(h, mlstm_w_in[j], mlstm_b_i[j], mlstm_b_f[j], mlstm_norm_g[j], mlstm_w_out[j])
        elif kind == 2:
            h = retention(h, positions, ret_w_in[j], ret_norm_g[j], ret_w_out[j])
        else:
            h = rglru_block(h, lru_w_in[j], lru_conv_w[j], lru_conv_b[j], lru_w_r[j], lru_b_r[j],
                            lru_w_i[j], lru_b_i[j], lru_lambda[j], lru_w_out[j])
        x = x + rmsnorm(h, norm_g[layer, 1])
        h = squared_relu_mlp(rmsnorm(x, norm_g[layer, 2]), mlp_w_up[layer], mlp_w_down[layer])
        x = x + rmsnorm(h, norm_g[layer, 3])
    return x
```

```python
import functools
import math

import numpy as np
import jax
import jax.numpy as jnp
from jax import lax
from jax.experimental import pallas as pl
from jax.experimental.pallas import tpu as pltpu

F32 = jnp.float32
BF16 = jnp.bfloat16

D_MODEL = 2048
D_FF = 4 * D_MODEL
NORM_EPS = 1e-6
CONV_WIDTH = 4
HALO_ROWS = 8

GDN_HEADS, GDN_DK, GDN_DV = 16, 128, 128
GDN_CHUNK = 64
MLSTM_HEADS, MLSTM_DQK, MLSTM_DV = 8, 128, 256
MLSTM_CHUNK = 128
GATE_SOFTCAP = 15.0
RET_HEADS, RET_DK, RET_DV = 8, 256, 512
RET_CHUNK = 128
ROPE_BASE = 10000.0
LRU_WIDTH, LRU_BLOCKS, LRU_BLOCK = 2048, 16, 128
LRU_ROWS = 128
LRU_C = 8.0

VMEM_LIMIT_BYTES = 56 * 1024 * 1024

NT_DIMS = (((1,), (1,)), ((), ()))
TN_DIMS = (((0,), (0,)), ((), ()))


def _params(*semantics):
    return pltpu.CompilerParams(dimension_semantics=semantics,
                                vmem_limit_bytes=VMEM_LIMIT_BYTES)


def _rms(x, g):
    y = x * lax.rsqrt(jnp.mean(x * x, axis=-1, keepdims=True) + NORM_EPS)
    return y * g


def _sigmoid(x):
    return 1.0 / (1.0 + jnp.exp(-x))


def _silu(x):
    return x * _sigmoid(x)


def _softplus(x):
    return jnp.maximum(x, 0.0) + jnp.log1p(jnp.exp(-jnp.abs(x)))


def _mxu(a, b, dims=None):
    a = a.astype(BF16)
    b = b.astype(BF16)
    if dims is None:
        return jnp.dot(a, b, preferred_element_type=F32)
    return lax.dot_general(a, b, dims, preferred_element_type=F32)


def _split_bf16(a):
    hi = a.astype(BF16)
    lo = (a - hi.astype(F32)).astype(BF16)
    return hi, lo


def _mxu3(a, b):
    m = a.shape[0]
    a_hi, a_lo = _split_bf16(a)
    b_hi, b_lo = _split_bf16(b)
    top = jnp.dot(jnp.concatenate([a_hi, a_lo], axis=0), b_hi, preferred_element_type=F32)
    return top[:m] + top[m:] + jnp.dot(a_hi, b_lo, preferred_element_type=F32)


def _tri_masks(n):
    row = lax.broadcasted_iota(jnp.int32, (n, n), 0)
    col = lax.broadcasted_iota(jnp.int32, (n, n), 1)
    return row >= col, row > col


def _cumsum_both(col_vals, row_vals, n):
    row = lax.broadcasted_iota(jnp.int32, (n, n), 0)
    col = lax.broadcasted_iota(jnp.int32, (n, n), 1)
    lower = jnp.where(row >= col, 1.0, 0.0).astype(F32)
    upper = jnp.where(row <= col, 1.0, 0.0).astype(F32)
    return _mxu3(lower, col_vals), _mxu3(row_vals, upper)


def _prenorm_kernel(x_ref, g_ref, o_ref):
    o_ref[...] = _rms(x_ref[...], g_ref[...]).astype(o_ref.dtype)


def _prenorm(x, g):
    s = x.shape[0]
    tm = 512
    return pl.pallas_call(
        _prenorm_kernel,
        grid=(s // tm,),
        in_specs=[pl.BlockSpec((tm, D_MODEL), lambda i: (i, 0)),
                  pl.BlockSpec((1, D_MODEL), lambda i: (0, 0))],
        out_specs=pl.BlockSpec((tm, D_MODEL), lambda i: (i, 0)),
        out_shape=jax.ShapeDtypeStruct((s, D_MODEL), BF16),
        compiler_params=_params("parallel"),
        name="prenorm",
    )(x, g.reshape(1, D_MODEL))


def _mm_cols_kernel(x_ref, w_ref, o_ref, wb_ref, *, act):
    @pl.when(pl.program_id(1) == 0)
    def _():
        wb_ref[...] = w_ref[...].astype(BF16)

    acc = jnp.dot(x_ref[...], wb_ref[...], preferred_element_type=F32)
    if act == "relu2":
        acc = jnp.square(jnp.maximum(acc, 0.0))
    o_ref[...] = acc.astype(o_ref.dtype)


def _mm_cols(xb, w, layer, n_cols, act=None, out_dtype=F32, tm=1024, tn=1024):
    m, k = xb.shape
    tm = min(tm, m)
    return pl.pallas_call(
        functools.partial(_mm_cols_kernel, act=act),
        grid=(n_cols // tn, m // tm),
        in_specs=[pl.BlockSpec((tm, k), lambda j, i: (i, 0)),
                  pl.BlockSpec((None, k, tn), lambda j, i: (layer, 0, j))],
        out_specs=pl.BlockSpec((tm, tn), lambda j, i: (i, j)),
        out_shape=jax.ShapeDtypeStruct((m, n_cols), out_dtype),
        scratch_shapes=[pltpu.VMEM((k, tn), BF16)],
        compiler_params=_params("arbitrary", "arbitrary"),
        name="mm_cols",
    )(xb, w)


def _tail_kernel(x_ref, w_ref, o_ref):
    o_ref[...] = jnp.dot(x_ref[...], w_ref[...].astype(BF16), preferred_element_type=F32)


def _mm_tail(xb, w_tail):
    m, k = xb.shape
    nt = w_tail.shape[1]
    tm = min(512, m)
    return pl.pallas_call(
        _tail_kernel,
        grid=(m // tm,),
        in_specs=[pl.BlockSpec((tm, k), lambda i: (i, 0)),
                  pl.BlockSpec((k, nt), lambda i: (0, 0))],
        out_specs=pl.BlockSpec((tm, nt), lambda i: (i, 0)),
        out_shape=jax.ShapeDtypeStruct((m, nt), F32),
        compiler_params=_params("parallel"),
        name="mm_tail",
    )(xb, w_tail)


def _chunk_rows(col_vals, n):
    s, nh = col_vals.shape
    return jnp.swapaxes(col_vals.reshape(s // n, n, nh), 1, 2)


def _mm_rows_kernel(a_ref, w_ref, x_ref, gp_ref, gn_ref, *refs, nk, with_next):
    if with_next:
        xo_ref, hn_ref, acc_ref = refs
    else:
        xo_ref, acc_ref = refs
    kk = pl.program_id(1)

    @pl.when(kk == 0)
    def _():
        acc_ref[...] = jnp.zeros_like(acc_ref)

    acc_ref[...] += jnp.dot(a_ref[...], w_ref[...].astype(BF16), preferred_element_type=F32)

    @pl.when(kk == nk - 1)
    def _():
        x_new = x_ref[...] + _rms(acc_ref[...], gp_ref[...])
        xo_ref[...] = x_new
        if with_next:
            hn_ref[...] = _rms(x_new, gn_ref[...]).astype(BF16)


def _mm_rows(a, w, layer, x, g_post, g_next, tm=512, tk=512):
    m, k = a.shape
    tm = min(tm, m)
    nk = k // tk
    with_next = g_next is not None
    if g_next is None:
        g_next = g_post
    row_spec = pl.BlockSpec((tm, D_MODEL), lambda i, kk: (i, 0))
    vec_spec = pl.BlockSpec((1, D_MODEL), lambda i, kk: (0, 0))
    out_specs = [row_spec]
    out_shape = [jax.ShapeDtypeStruct((m, D_MODEL), F32)]
    if with_next:
        out_specs.append(row_spec)
        out_shape.append(jax.ShapeDtypeStruct((m, D_MODEL), BF16))
    res = pl.pallas_call(
        functools.partial(_mm_rows_kernel, nk=nk, with_next=with_next),
        grid=(m // tm, nk),
        in_specs=[pl.BlockSpec((tm, tk), lambda i, kk: (i, kk)),
                  pl.BlockSpec((None, tk, D_MODEL), lambda i, kk: (layer, kk, 0)),
                  row_spec, vec_spec, vec_spec],
        out_specs=out_specs,
        out_shape=out_shape,
        scratch_shapes=[pltpu.VMEM((tm, D_MODEL), F32)],
        compiler_params=_params("parallel", "arbitrary"),
        name="mm_rows",
    )(a, w, x, g_post.reshape(1, D_MODEL), g_next.reshape(1, D_MODEL))
    return (res[0], res[1]) if with_next else (res[0], None)


def _causal_conv(x, halo_ref, cw, first):
    rows = x.shape[0]

    @pl.when(first)
    def _():
        halo_ref[...] = jnp.zeros_like(halo_ref)

    xx = jnp.concatenate([halo_ref[...], x], axis=0)
    y = cw[0:1] * xx[HALO_ROWS - 3:HALO_ROWS - 3 + rows]
    y = y + cw[1:2] * xx[HALO_ROWS - 2:HALO_ROWS - 2 + rows]
    y = y + cw[2:3] * xx[HALO_ROWS - 1:HALO_ROWS - 1 + rows]
    y = y + cw[3:4] * x
    halo_ref[...] = x[rows - HALO_ROWS:]
    return y


def _unit_lower_inverse(a):
    n = a.shape[0]
    eye = jnp.where(lax.broadcasted_iota(jnp.int32, (n, n), 0) == lax.broadcasted_iota(jnp.int32, (n, n), 1),
                    1.0, 0.0).astype(F32)
    p = eye - a
    mpow = _mxu3(a, a)
    levels = int(math.log2(n)) - 2
    for _ in range(levels):
        both = _mxu3(jnp.concatenate([p, mpow], axis=0), mpow)
        p = p + both[:n]
        mpow = both[n:]
    return p + _mxu3(p, mpow)


def _gdn_kernel(qkv_ref, z_ref, b_ref, a_ref, at_ref, cw_ref, al_ref, dtb_ref, alt_ref, dtbt_ref, ng_ref,
                o_ref, halo_ref, act_ref, s_ref):
    n = GDN_CHUNK
    hdk = GDN_HEADS * GDN_DK
    first = pl.program_id(0) == 0

    @pl.when(first)
    def _():
        s_ref[...] = jnp.zeros_like(s_ref)

    y = _causal_conv(qkv_ref[...], halo_ref, cw_ref[...], first)
    act_ref[...] = _silu(y)

    beta = _sigmoid(b_ref[...])
    g_col = -jnp.exp(al_ref[...]) * _softplus(a_ref[...] + dtb_ref[...])
    g_row = -jnp.exp(alt_ref[...]) * _softplus(at_ref[...] + dtbt_ref[...])
    gc_col, gc_row = _cumsum_both(g_col, g_row, n)
    incl, strict = _tri_masks(n)
    ng = ng_ref[...]

    for h in range(GDN_HEADS):
        qh = act_ref[:, h * GDN_DK:(h + 1) * GDN_DK]
        kh = act_ref[:, hdk + h * GDN_DK:hdk + (h + 1) * GDN_DK]
        vh = act_ref[:, 2 * hdk + h * GDN_DV:2 * hdk + (h + 1) * GDN_DV]
        q = qh * lax.rsqrt(jnp.sum(qh * qh, axis=-1, keepdims=True) + NORM_EPS) * GDN_DK ** -0.5
        k = kh * lax.rsqrt(jnp.sum(kh * kh, axis=-1, keepdims=True) + NORM_EPS)
        gcc = gc_col[:, h:h + 1]
        gcr = gc_row[h:h + 1, :]
        bc = beta[:, h:h + 1]
        dec = jnp.where(incl, jnp.exp(gcc - gcr), 0.0)
        prod = _mxu(jnp.concatenate([k, q], axis=0), k, NT_DIMS)
        kkt, qkt = prod[:n], prod[n:]
        a_mat = jnp.where(strict, bc * kkt * dec, 0.0)
        t_inv = _unit_lower_inverse(a_mat)
        eg = jnp.exp(gcc)
        rhs = jnp.concatenate([vh * bc, k * (bc * eg)], axis=1)
        uw = _mxu3(t_inv, rhs)
        u, w = uw[:, :GDN_DV], uw[:, GDN_DV:]
        attn = qkt * dec
        state = s_ref[h]
        qs_ws = _mxu(jnp.concatenate([q * eg, w], axis=0), state)
        v_new = u - qs_ws[n:]
        o = qs_ws[:n] + _mxu(attn, v_new)
        g_last = gcr[:, n - 1:n]
        s_ref[h] = state * jnp.exp(g_last) + _mxu(k * jnp.exp(g_last - gcc), v_new, TN_DIMS)
        zh = z_ref[:, h * GDN_DV:(h + 1) * GDN_DV]
        o_ref[:, h * GDN_DV:(h + 1) * GDN_DV] = (_rms(o, ng) * _silu(zh)).astype(BF16)


def _gdn_mixer(hn, w_in, conv_w, a_log, dt_bias, norm_g):
    s = hn.shape[0]
    n = GDN_CHUNK
    nh = GDN_HEADS
    c_qkv = 2 * nh * GDN_DK + nh * GDN_DV
    c_z = nh * GDN_DV
    w2 = w_in.reshape(1, D_MODEL, -1)
    proj = _mm_cols(hn, w2, 0, c_qkv + c_z)
    tail = _mm_tail(hn, w_in[:, c_qkv + c_z:])
    b_pre, a_pre = tail[:, :nh], tail[:, nh:]
    a_pre_t = _chunk_rows(a_pre, n)
    full = lambda shape: pl.BlockSpec(shape, lambda i: (0, 0))
    return pl.pallas_call(
        _gdn_kernel,
        grid=(s // n,),
        in_specs=[pl.BlockSpec((n, c_qkv), lambda i: (i, 0)),
                  pl.BlockSpec((n, c_z), lambda i: (i, c_qkv // c_z)),
                  pl.BlockSpec((n, nh), lambda i: (i, 0)),
                  pl.BlockSpec((n, nh), lambda i: (i, 0)),
                  pl.BlockSpec((None, nh, n), lambda i: (i, 0, 0)),
                  full((CONV_WIDTH, c_qkv)),
                  full((1, nh)), full((1, nh)), full((nh, 1)), full((nh, 1)),
                  full((1, GDN_DV))],
        out_specs=pl.BlockSpec((n, c_z), lambda i: (i, 0)),
        out_shape=jax.ShapeDtypeStruct((s, c_z), BF16),
        scratch_shapes=[pltpu.VMEM((HALO_ROWS, c_qkv), F32),
                        pltpu.VMEM((n, c_qkv), F32),
                        pltpu.VMEM((nh, GDN_DK, GDN_DV), F32)],
        compiler_params=_params("arbitrary"),
        name="gdn_chunks",
    )(proj, proj, b_pre, a_pre, a_pre_t, conv_w,
      a_log.reshape(1, nh), dt_bias.reshape(1, nh), a_log.reshape(nh, 1), dt_bias.reshape(nh, 1),
      norm_g.reshape(1, GDN_DV))


def _softcap(t):
    return GATE_SOFTCAP * jnp.tanh(t / GATE_SOFTCAP)


def _mlstm_kernel(p_ref, i_ref, f_ref, it_ref, ft_ref, bi_ref, bf_ref, bit_ref, bft_ref, ng_ref,
                  o_ref, c_ref, n_ref, m_ref):
    n = MLSTM_CHUNK
    nh, dqk, dv = MLSTM_HEADS, MLSTM_DQK, MLSTM_DV

    @pl.when(pl.program_id(0) == 0)
    def _():
        c_ref[...] = jnp.zeros_like(c_ref)
        n_ref[...] = jnp.zeros_like(n_ref)
        m_ref[...] = jnp.zeros_like(m_ref)

    i_col = _softcap(i_ref[...] + bi_ref[...])
    i_row = _softcap(it_ref[...] + bit_ref[...])
    f_col = -_softplus(-_softcap(f_ref[...] + bf_ref[...]))
    f_row = -_softplus(-_softcap(ft_ref[...] + bft_ref[...]))
    b_col, b_row = _cumsum_both(f_col, f_row, n)
    incl, _ = _tri_masks(n)

    for h in range(nh):
        q = p_ref[:, h * dqk:(h + 1) * dqk] * dqk ** -0.5
        k = p_ref[:, nh * dqk + h * dqk:nh * dqk + (h + 1) * dqk]
        v = p_ref[:, 2 * nh * dqk + h * dv:2 * nh * dqk + (h + 1) * dv]
        bcc, bcr = b_col[:, h:h + 1], b_row[h:h + 1, :]
        icc, icr = i_col[:, h:h + 1], i_row[h:h + 1, :]
        m_prev = m_ref[h:h + 1, 0:1]
        d_log = jnp.where(incl, bcc - bcr + icr, -jnp.inf)
        d_max = jnp.max(d_log, axis=-1, keepdims=True)
        inter = bcc + m_prev
        m_t = jnp.maximum(inter, d_max)
        sc = _mxu(q, k, NT_DIMS) * jnp.exp(d_log - m_t)
        w_inter = jnp.exp(inter - m_t)
        c_mem = c_ref[h]
        n_mem = n_ref[h:h + 1, :]
        num = w_inter * _mxu(q, c_mem) + _mxu(sc, v)
        den = w_inter * jnp.sum(q * n_mem, axis=-1, keepdims=True) + jnp.sum(sc, axis=-1, keepdims=True)
        hid = num / jnp.maximum(jnp.abs(den), jnp.exp(-m_t))
        b_last = bcr[:, n - 1:n]
        end_max = jnp.max(b_last - bcr + icr, axis=-1, keepdims=True)
        m_new = jnp.maximum(b_last + m_prev, end_max)
        decay = jnp.exp(b_last + m_prev - m_new)
        k_w = k * jnp.exp(b_last - bcc + icc - m_new)
        c_ref[h] = decay * c_mem + _mxu(k_w, v, TN_DIMS)
        n_ref[h:h + 1, :] = decay * n_mem + jnp.sum(k_w, axis=0, keepdims=True)
        m_ref[h:h + 1, :] = jnp.broadcast_to(m_new, (1, m_ref.shape[1]))
        o_pre = p_ref[:, 2 * nh * dqk + nh * dv + h * dv:2 * nh * dqk + nh * dv + (h + 1) * dv]
        hid = _rms(hid, ng_ref[:, h * dv:(h + 1) * dv])
        o_ref[:, h * dv:(h + 1) * dv] = (hid * _sigmoid(o_pre)).astype(BF16)


def _mlstm_mixer(hn, w_in, b_i, b_f, norm_g):
    s = hn.shape[0]
    n = min(MLSTM_CHUNK, s)
    nh, dqk, dv = MLSTM_HEADS, MLSTM_DQK, MLSTM_DV
    c_main = 2 * nh * dqk + 2 * nh * dv
    proj = _mm_cols(hn, w_in.reshape(1, D_MODEL, -1), 0, c_main)
    tail = _mm_tail(hn, w_in[:, c_main:])
    i_pre, f_pre = tail[:, :nh], tail[:, nh:]
    full = lambda shape: pl.BlockSpec(shape, lambda i: (0, 0))
    col = pl.BlockSpec((n, nh), lambda i: (i, 0))
    row = pl.BlockSpec((None, nh, n), lambda i: (i, 0, 0))
    return pl.pallas_call(
        _mlstm_kernel,
        grid=(s // n,),
        in_specs=[pl.BlockSpec((n, c_main), lambda i: (i, 0)), col, col, row, row,
                  full((1, nh)), full((1, nh)), full((nh, 1)), full((nh, 1)), full((1, nh * dv))],
        out_specs=pl.BlockSpec((n, nh * dv), lambda i: (i, 0)),
        out_shape=jax.ShapeDtypeStruct((s, nh * dv), BF16),
        scratch_shapes=[pltpu.VMEM((nh, dqk, dv), F32),
                        pltpu.VMEM((nh, dqk), F32),
                        pltpu.VMEM((nh, 128), F32)],
        compiler_params=_params("arbitrary"),
        name="mlstm_chunks",
    )(proj, i_pre, f_pre, _chunk_rows(i_pre, n), _chunk_rows(f_pre, n),
      b_i.reshape(1, nh), b_f.reshape(1, nh), b_i.reshape(nh, 1), b_f.reshape(nh, 1),
      norm_g.reshape(1, nh * dv))


def _ret_kernel(p_ref, pos_ref, fr_ref, ng_ref, o_ref, r_ref):
    n = RET_CHUNK
    nh, dk, dv = RET_HEADS, RET_DK, RET_DV
    half = dk // 2

    @pl.when(pl.program_id(0) == 0)
    def _():
        r_ref[...] = jnp.zeros_like(r_ref)

    ang = pos_ref[...].astype(F32) * fr_ref[...]
    cos, sin = jnp.cos(ang), jnp.sin(ang)
    row = lax.broadcasted_iota(jnp.int32, (n, n), 0)
    col = lax.broadcasted_iota(jnp.int32, (n, n), 1)
    lag = (row - col).astype(F32)
    t_col = lax.broadcasted_iota(jnp.int32, (n, 1), 0).astype(F32)

    def rope(t):
        t1, t2 = t[:, :half], t[:, half:]
        return jnp.concatenate([t1 * cos - t2 * sin, t1 * sin + t2 * cos], axis=1)

    for h in range(nh):
        log_gamma = float(np.log1p(-np.exp2(np.float32(-5.0 - h))))
        q = rope(p_ref[:, h * dk:(h + 1) * dk])
        k = rope(p_ref[:, nh * dk + h * dk:nh * dk + (h + 1) * dk]) * dk ** -0.5
        v = p_ref[:, 2 * nh * dk + h * dv:2 * nh * dk + (h + 1) * dv]
        d_mat = jnp.where(row >= col, jnp.exp(lag * log_gamma), 0.0)
        xi = jnp.exp((t_col + 1.0) * log_gamma)
        zeta = jnp.exp((n - 1.0 - t_col) * log_gamma)
        mem = r_ref[h]
        o = _mxu(_mxu(q, k, NT_DIMS) * d_mat, v) + _mxu(q, mem) * xi
        r_ref[h] = mem * math.exp(n * log_gamma) + _mxu(k * zeta, v, TN_DIMS)
        gate = p_ref[:, 2 * nh * dk + nh * dv + h * dv:2 * nh * dk + nh * dv + (h + 1) * dv]
        o = _rms(o, ng_ref[:, h * dv:(h + 1) * dv])
        o_ref[:, h * dv:(h + 1) * dv] = (o * _silu(gate)).astype(BF16)


def _ret_mixer(hn, positions, w_in, norm_g):
    s = hn.shape[0]
    n = min(RET_CHUNK, s)
    nh, dk, dv = RET_HEADS, RET_DK, RET_DV
    c_all = 2 * nh * dk + 2 * nh * dv
    proj = _mm_cols(hn, w_in.reshape(1, D_MODEL, -1), 0, c_all)
    freqs = ROPE_BASE ** (-jnp.arange(0, dk, 2, dtype=F32) / dk)
    return pl.pallas_call(
        _ret_kernel,
        grid=(s // n,),
        in_specs=[pl.BlockSpec((n, c_all), lambda i: (i, 0)),
                  pl.BlockSpec((n, 1), lambda i: (i, 0)),
                  pl.BlockSpec((1, dk // 2), lambda i: (0, 0)),
                  pl.BlockSpec((1, nh * dv), lambda i: (0, 0))],
        out_specs=pl.BlockSpec((n, nh * dv), lambda i: (i, 0)),
        out_shape=jax.ShapeDtypeStruct((s, nh * dv), BF16),
        scratch_shapes=[pltpu.VMEM((nh, dk, dv), F32)],
        compiler_params=_params("arbitrary"),
        name="ret_chunks",
    )(proj, positions.reshape(s, 1), freqs.reshape(1, dk // 2), norm_g.reshape(1, nh * dv))


def _gelu_tanh(x):
    return 0.5 * x * (1.0 + jnp.tanh(math.sqrt(2.0 / math.pi) * (x + 0.044715 * (x * x * x))))


def _lru_kernel(p_ref, cw_ref, cb_ref, wr_ref, br_ref, wi_ref, bi_ref, lam_ref, o_ref, halo_ref, h_ref):
    rows = p_ref.shape[0]
    first = pl.program_id(0) == 0

    @pl.when(first)
    def _():
        h_ref[...] = jnp.zeros_like(h_ref)

    xb = _causal_conv(p_ref[:, :LRU_WIDTH], halo_ref, cw_ref[...], first) + cb_ref[...]
    neg_c_softplus = -LRU_C * _softplus(-lam_ref[...])
    t_idx = lax.broadcasted_iota(jnp.int32, (rows, LRU_BLOCK), 0)

    for b in range(LRU_BLOCKS):
        lanes = slice(b * LRU_BLOCK, (b + 1) * LRU_BLOCK)
        xblk = xb[:, lanes]
        r = _sigmoid(_mxu(xblk, wr_ref[b]) + br_ref[:, lanes])
        gi = _sigmoid(_mxu(xblk, wi_ref[b]) + bi_ref[:, lanes])
        log_a = r * neg_c_softplus[:, lanes]
        a = jnp.exp(log_a)
        u = jnp.sqrt(1.0 - jnp.exp(2.0 * log_a)) * (gi * xblk)
        d = 1
        while d < rows:
            keep = t_idx >= d
            u = u + a * jnp.where(keep, pltpu.roll(u, d, axis=0), 0.0)
            a = a * jnp.where(keep, pltpu.roll(a, d, axis=0), 1.0)
            d *= 2
        hs = u + a * h_ref[0:1, lanes]
        h_ref[0:1, lanes] = hs[rows - 1:rows]
        gate = p_ref[:, LRU_WIDTH + b * LRU_BLOCK:LRU_WIDTH + (b + 1) * LRU_BLOCK]
        o_ref[:, lanes] = (hs * _gelu_tanh(gate)).astype(BF16)


def _lru_mixer(hn, w_in, conv_w, conv_b, w_r, b_r, w_i, b_i, lam):
    s = hn.shape[0]
    rows = min(LRU_ROWS, s)
    w = LRU_WIDTH
    proj = _mm_cols(hn, w_in.reshape(1, D_MODEL, -1), 0, 2 * w)
    vec = pl.BlockSpec((1, w), lambda i: (0, 0))
    blk = pl.BlockSpec((LRU_BLOCKS, LRU_BLOCK, LRU_BLOCK), lambda i: (0, 0, 0))
    return pl.pallas_call(
        _lru_kernel,
        grid=(s // rows,),
        in_specs=[pl.BlockSpec((rows, 2 * w), lambda i: (i, 0)),
                  pl.BlockSpec((CONV_WIDTH, w), lambda i: (0, 0)), vec, blk, vec, blk, vec, vec],
        out_specs=pl.BlockSpec((rows, w), lambda i: (i, 0)),
        out_shape=jax.ShapeDtypeStruct((s, w), BF16),
        scratch_shapes=[pltpu.VMEM((HALO_ROWS, w), F32), pltpu.VMEM((8, w), F32)],
        compiler_params=_params("arbitrary"),
        name="lru_scan",
    )(proj, conv_w, conv_b.reshape(1, w), w_r, b_r.reshape(1, w), w_i, b_i.reshape(1, w), lam.reshape(1, w))


@jax.jit
def kernel(x, positions, norm_g, mlp_w_up, mlp_w_down, gdn_w_in, gdn_conv_w, gdn_a_log, gdn_dt_bias, gdn_norm_g, gdn_w_out, mlstm_w_in, mlstm_b_i, mlstm_b_f, mlstm_norm_g, mlstm_w_out, ret_w_in, ret_norm_g, ret_w_out, lru_w_in, lru_conv_w, lru_conv_b, lru_w_r, lru_b_r, lru_w_i, lru_b_i, lru_lambda, lru_w_out):
    batch, seq, _ = x.shape
    assert batch == 1
    depth = norm_g.shape[0]
    xs = x.reshape(seq, D_MODEL)
    hn = _prenorm(xs, norm_g[0, 0])
    for layer in range(depth):
        kind, j = layer % 4, layer // 4
        if kind == 0:
            mixed = _gdn_mixer(hn, gdn_w_in[j], gdn_conv_w[j], gdn_a_log[j], gdn_dt_bias[j], gdn_norm_g[j])
            w_out = gdn_w_out
        elif kind == 1:
            mixed = _mlstm_mixer(hn, mlstm_w_in[j], mlstm_b_i[j], mlstm_b_f[j], mlstm_norm_g[j])
            w_out = mlstm_w_out
        elif kind == 2:
            mixed = _ret_mixer(hn, positions, ret_w_in[j], ret_norm_g[j])
            w_out = ret_w_out
        else:
            mixed = _lru_mixer(hn, lru_w_in[j], lru_conv_w[j], lru_conv_b[j], lru_w_r[j], lru_b_r[j],
                               lru_w_i[j], lru_b_i[j], lru_lambda[j])
            w_out = lru_w_out
        xs, hn = _mm_rows(mixed, w_out, j, xs, norm_g[layer, 1], norm_g[layer, 2])
        hmid = _mm_cols(hn, mlp_w_up, layer, D_FF, act="relu2", out_dtype=BF16)
        g_next = norm_g[layer + 1, 0] if layer + 1 < depth else None
        xs, hn = _mm_rows(hmid, mlp_w_down, layer, xs, norm_g[layer, 3], g_next)
    return xs.reshape(batch, seq, D_MODEL)
```

```python
import functools
import math

import numpy as np
import jax
import jax.numpy as jnp
from jax import lax
from jax.experimental import pallas as pl
from jax.experimental.pallas import tpu as pltpu

F32 = jnp.float32
BF16 = jnp.bfloat16

D_MODEL = 2048
D_FF = 4 * D_MODEL
NORM_EPS = 1e-6
CONV_WIDTH = 4
HALO_ROWS = 8

GDN_HEADS, GDN_DK, GDN_DV = 16, 128, 128
GDN_CHUNK = 64
MLSTM_HEADS, MLSTM_DQK, MLSTM_DV = 8, 128, 256
MLSTM_CHUNK = 128
GATE_SOFTCAP = 15.0
RET_HEADS, RET_DK, RET_DV = 8, 256, 512
RET_CHUNK = 128
ROPE_BASE = 10000.0
LRU_WIDTH, LRU_BLOCKS, LRU_BLOCK = 2048, 16, 128
LRU_ROWS = 128
LRU_C = 8.0

VMEM_LIMIT_BYTES = 56 * 1024 * 1024

NT_DIMS = (((1,), (1,)), ((), ()))
TN_DIMS = (((0,), (0,)), ((), ()))


def _params(*semantics):
    return pltpu.CompilerParams(dimension_semantics=semantics,
                                vmem_limit_bytes=VMEM_LIMIT_BYTES)


def _rms(x, g):
    y = x * lax.rsqrt(jnp.mean(x * x, axis=-1, keepdims=True) + NORM_EPS)
    return y * g


def _sigmoid(x):
    return 1.0 / (1.0 + jnp.exp(-x))


def _silu(x):
    return x * _sigmoid(x)


def _softplus(x):
    return jnp.maximum(x, 0.0) + jnp.log1p(jnp.exp(-jnp.abs(x)))


def _mxu(a, b, dims=None):
    a = a.astype(BF16)
    b = b.astype(BF16)
    if dims is None:
        return jnp.dot(a, b, preferred_element_type=F32)
    return lax.dot_general(a, b, dims, preferred_element_type=F32)


def _split_bf16(a):
    hi = a.astype(BF16)
    lo = (a - hi.astype(F32)).astype(BF16)
    return hi, lo


def _mxu3(a, b):
    m = a.shape[0]
    a_hi, a_lo = _split_bf16(a)
    b_hi, b_lo = _split_bf16(b)
    top = jnp.dot(jnp.concatenate([a_hi, a_lo], axis=0), b_hi, preferred_element_type=F32)
    return top[:m] + top[m:] + jnp.dot(a_hi, b_lo, preferred_element_type=F32)


def _tri_masks(n):
    row = lax.broadcasted_iota(jnp.int32, (n, n), 0)
    col = lax.broadcasted_iota(jnp.int32, (n, n), 1)
    return row >= col, row > col


def _cumsum_both(col_vals, row_vals, n):
    row = lax.broadcasted_iota(jnp.int32, (n, n), 0)
    col = lax.broadcasted_iota(jnp.int32, (n, n), 1)
    lower = jnp.where(row >= col, 1.0, 0.0).astype(F32)
    upper = jnp.where(row <= col, 1.0, 0.0).astype(F32)
    return _mxu3(lower, col_vals), _mxu3(row_vals, upper)


def _prenorm_kernel(x_ref, g_ref, o_ref):
    o_ref[...] = _rms(x_ref[...], g_ref[...]).astype(o_ref.dtype)


def _prenorm(x, g):
    s = x.shape[0]
    tm = 512
    return pl.pallas_call(
        _prenorm_kernel,
        grid=(s // tm,),
        in_specs=[pl.BlockSpec((tm, D_MODEL), lambda i: (i, 0)),
                  pl.BlockSpec((1, D_MODEL), lambda i: (0, 0))],
        out_specs=pl.BlockSpec((tm, D_MODEL), lambda i: (i, 0)),
        out_shape=jax.ShapeDtypeStruct((s, D_MODEL), BF16),
        compiler_params=_params("parallel"),
        name="prenorm",
    )(x, g.reshape(1, D_MODEL))


def _mm_cols_kernel(x_ref, w_ref, o_ref, wb_ref, *, act):
    @pl.when(pl.program_id(1) == 0)
    def _():
        wb_ref[...] = w_ref[...].astype(BF16)

    acc = jnp.dot(x_ref[...], wb_ref[...], preferred_element_type=F32)
    if act == "relu2":
        acc = jnp.square(jnp.maximum(acc, 0.0))
    o_ref[...] = acc.astype(o_ref.dtype)


def _mm_cols(xb, w, layer, n_cols, act=None, out_dtype=F32, tm=1024, tn=1024):
    m, k = xb.shape
    tm = min(tm, m)
    return pl.pallas_call(
        functools.partial(_mm_cols_kernel, act=act),
        grid=(n_cols // tn, m // tm),
        in_specs=[pl.BlockSpec((tm, k), lambda j, i: (i, 0)),
                  pl.BlockSpec((None, k, tn), lambda j, i: (layer, 0, j))],
        out_specs=pl.BlockSpec((tm, tn), lambda j, i: (i, j)),
        out_shape=jax.ShapeDtypeStruct((m, n_cols), out_dtype),
        scratch_shapes=[pltpu.VMEM((k, tn), BF16)],
        compiler_params=_params("arbitrary", "arbitrary"),
        name="mm_cols",
    )(xb, w)


def _tail_kernel(x_ref, w_ref, o_ref):
    o_ref[...] = jnp.dot(x_ref[...], w_ref[...].astype(BF16), preferred_element_type=F32)


def _mm_tail(xb, w_tail):
    m, k = xb.shape
    nt = w_tail.shape[1]
    tm = min(512, m)
    return pl.pallas_call(
        _tail_kernel,
        grid=(m // tm,),
        in_specs=[pl.BlockSpec((tm, k), lambda i: (i, 0)),
                  pl.BlockSpec((k, nt), lambda i: (0, 0))],
        out_specs=pl.BlockSpec((tm, nt), lambda i: (i, 0)),
        out_shape=jax.ShapeDtypeStruct((m, nt), F32),
        compiler_params=_params("parallel"),
        name="mm_tail",
    )(xb, w_tail)


def _chunk_rows(col_vals, n):
    s, nh = col_vals.shape
    return jnp.swapaxes(col_vals.reshape(s // n, n, nh), 1, 2)


def _cast_kernel(w_ref, o_ref):
    o_ref[...] = w_ref[...].astype(o_ref.dtype)


def _to_bf16(w, tk=512):
    nl, k, n = w.shape
    spec = pl.BlockSpec((None, tk, n), lambda l, kk: (l, kk, 0))
    return pl.pallas_call(
        _cast_kernel,
        grid=(nl, k // tk),
        in_specs=[spec],
        out_specs=spec,
        out_shape=jax.ShapeDtypeStruct(w.shape, BF16),
        compiler_params=_params("parallel", "parallel"),
        name="to_bf16",
    )(w)


def _mm_rows_kernel(a_ref, w_ref, x_ref, gp_ref, gn_ref, *refs, nk, with_next):
    xo_ref = refs[0]
    kk = pl.program_id(1)

    @pl.when(kk == 0)
    def _():
        xo_ref[...] = jnp.zeros_like(xo_ref)

    xo_ref[...] += jnp.dot(a_ref[...], w_ref[...], preferred_element_type=F32)

    @pl.when(kk == nk - 1)
    def _():
        x_new = x_ref[...] + _rms(xo_ref[...], gp_ref[...])
        xo_ref[...] = x_new
        if with_next:
            refs[1][...] = _rms(x_new, gn_ref[...]).astype(BF16)


def _mm_rows(a, w, layer, x, g_post, g_next, tm=1024, tk=512):
    m, k = a.shape
    tm = min(tm, m)
    nk = k // tk
    with_next = g_next is not None
    if g_next is None:
        g_next = g_post
    row_spec = pl.BlockSpec((tm, D_MODEL), lambda i, kk: (i, 0))
    vec_spec = pl.BlockSpec((1, D_MODEL), lambda i, kk: (0, 0))
    out_specs = [row_spec]
    out_shape = [jax.ShapeDtypeStruct((m, D_MODEL), F32)]
    if with_next:
        out_specs.append(row_spec)
        out_shape.append(jax.ShapeDtypeStruct((m, D_MODEL), BF16))
    res = pl.pallas_call(
        functools.partial(_mm_rows_kernel, nk=nk, with_next=with_next),
        grid=(m // tm, nk),
        in_specs=[pl.BlockSpec((tm, tk), lambda i, kk: (i, kk)),
                  pl.BlockSpec((None, tk, D_MODEL), lambda i, kk: (layer, kk, 0)),
                  row_spec, vec_spec, vec_spec],
        out_specs=out_specs,
        out_shape=out_shape,
        compiler_params=_params("parallel", "arbitrary"),
        name="mm_rows",
    )(a, w, x, g_post.reshape(1, D_MODEL), g_next.reshape(1, D_MODEL))
    return (res[0], res[1]) if with_next else (res[0], None)


def _causal_conv(x, halo_ref, cw, first):
    rows = x.shape[0]

    @pl.when(first)
    def _():
        halo_ref[...] = jnp.zeros_like(halo_ref)

    xx = jnp.concatenate([halo_ref[...], x], axis=0)
    y = cw[0:1] * xx[HALO_ROWS - 3:HALO_ROWS - 3 + rows]
    y = y + cw[1:2] * xx[HALO_ROWS - 2:HALO_ROWS - 2 + rows]
    y = y + cw[2:3] * xx[HALO_ROWS - 1:HALO_ROWS - 1 + rows]
    y = y + cw[3:4] * x
    halo_ref[...] = x[rows - HALO_ROWS:]
    return y


GDN_HEAD_GROUP = 8


def _unit_lower_inverses(mats):
    n = mats[0].shape[0]
    row = lax.broadcasted_iota(jnp.int32, (n, n), 0)
    col = lax.broadcasted_iota(jnp.int32, (n, n), 1)
    eye = jnp.where(row == col, 1.0, 0.0).astype(F32)

    def same_block(b):
        shift = int(math.log2(b))
        return (row >> shift) == (col >> shift)

    b = 2
    ds = [eye - jnp.where(same_block(b), a, 0.0) for a in mats]
    while b < n:
        inner, outer = same_block(b), same_block(2 * b)
        es = [jnp.where(outer, jnp.where(inner, 0.0, a), 0.0) for a in mats]
        eds = [_mxu(e, d) for e, d in zip(es, ds)]
        ds = [d - _mxu(d, ed) for d, ed in zip(ds, eds)]
        b *= 2
    return ds


def _gdn_kernel(qkv_ref, z_ref, b_ref, a_ref, at_ref, cw_ref, al_ref, dtb_ref, alt_ref, dtbt_ref, ng_ref,
                o_ref, halo_ref, act_ref, s_ref):
    n = GDN_CHUNK
    hdk = GDN_HEADS * GDN_DK
    first = pl.program_id(0) == 0

    @pl.when(first)
    def _():
        s_ref[...] = jnp.zeros_like(s_ref)

    y = _causal_conv(qkv_ref[...], halo_ref, cw_ref[...], first)
    act_ref[...] = _silu(y)

    beta = _sigmoid(b_ref[...])
    g_col = -jnp.exp(al_ref[...]) * _softplus(a_ref[...] + dtb_ref[...])
    g_row = -jnp.exp(alt_ref[...]) * _softplus(at_ref[...] + dtbt_ref[...])
    gc_col, gc_row = _cumsum_both(g_col, g_row, n)
    incl, strict = _tri_masks(n)
    ng = ng_ref[...]

    def l2norm(t):
        return t * lax.rsqrt(jnp.sum(t * t, axis=-1, keepdims=True) + NORM_EPS)

    for g0 in range(0, GDN_HEADS, GDN_HEAD_GROUP):
        heads = range(g0, g0 + GDN_HEAD_GROUP)
        q = [l2norm(act_ref[:, h * GDN_DK:(h + 1) * GDN_DK]) * GDN_DK ** -0.5 for h in heads]
        k = [l2norm(act_ref[:, hdk + h * GDN_DK:hdk + (h + 1) * GDN_DK]) for h in heads]
        v = [act_ref[:, 2 * hdk + h * GDN_DV:2 * hdk + (h + 1) * GDN_DV] for h in heads]
        gcc = [gc_col[:, h:h + 1] for h in heads]
        gcr = [gc_row[h:h + 1, :] for h in heads]
        bc = [beta[:, h:h + 1] for h in heads]
        dec = [jnp.where(incl, jnp.exp(c - r), 0.0) for c, r in zip(gcc, gcr)]
        prod = [_mxu(jnp.concatenate([ki, qi], axis=0), ki, NT_DIMS) for ki, qi in zip(k, q)]
        a_mat = [jnp.where(strict, b * p[:n] * d, 0.0) for b, p, d in zip(bc, prod, dec)]
        t_inv = _unit_lower_inverses(a_mat)
        eg = [jnp.exp(c) for c in gcc]
        rhs = [jnp.concatenate([vi * b, ki * (b * e)], axis=1) for vi, ki, b, e in zip(v, k, bc, eg)]
        uw = [_mxu3(t, r) for t, r in zip(t_inv, rhs)]
        state = [s_ref[h] for h in heads]
        qs_ws = [_mxu(jnp.concatenate([qi * e, x[:, GDN_DV:]], axis=0), st)
                 for qi, e, x, st in zip(q, eg, uw, state)]
        v_new = [x[:, :GDN_DV] - y[n:] for x, y in zip(uw, qs_ws)]
        o = [y[:n] + _mxu(p[n:] * d, vn) for y, p, d, vn in zip(qs_ws, prod, dec, v_new)]
        g_last = [r[:, n - 1:n] for r in gcr]
        upd = [_mxu(ki * jnp.exp(gl - c), vn, TN_DIMS) for ki, gl, c, vn in zip(k, g_last, gcc, v_new)]
        for i, h in enumerate(heads):
            s_ref[h] = state[i] * jnp.exp(g_last[i]) + upd[i]
            zh = z_ref[:, h * GDN_DV:(h + 1) * GDN_DV]
            o_ref[:, h * GDN_DV:(h + 1) * GDN_DV] = (_rms(o[i], ng) * _silu(zh)).astype(BF16)


def _gdn_mixer(hn, w_in, j, conv_w, a_log, dt_bias, norm_g):
    s = hn.shape[0]
    n = GDN_CHUNK
    nh = GDN_HEADS
    c_qkv = 2 * nh * GDN_DK + nh * GDN_DV
    c_z = nh * GDN_DV
    proj = _mm_cols(hn, w_in, j, c_qkv + c_z)
    tail = _mm_tail(hn, w_in[j, :, c_qkv + c_z:])
    b_pre, a_pre = tail[:, :nh], tail[:, nh:]
    a_pre_t = _chunk_rows(a_pre, n)
    full = lambda shape: pl.BlockSpec(shape, lambda i: (0, 0))
    return pl.pallas_call(
        _gdn_kernel,
        grid=(s // n,),
        in_specs=[pl.BlockSpec((n, c_qkv), lambda i: (i, 0)),
                  pl.BlockSpec((n, c_z), lambda i: (i, c_qkv // c_z)),
                  pl.BlockSpec((n, nh), lambda i: (i, 0)),
                  pl.BlockSpec((n, nh), lambda i: (i, 0)),
                  pl.BlockSpec((None, nh, n), lambda i: (i, 0, 0)),
                  full((CONV_WIDTH, c_qkv)),
                  full((1, nh)), full((1, nh)), full((nh, 1)), full((nh, 1)),
                  full((1, GDN_DV))],
        out_specs=pl.BlockSpec((n, c_z), lambda i: (i, 0)),
        out_shape=jax.ShapeDtypeStruct((s, c_z), BF16),
        scratch_shapes=[pltpu.VMEM((HALO_ROWS, c_qkv), F32),
                        pltpu.VMEM((n, c_qkv), F32),
                        pltpu.VMEM((nh, GDN_DK, GDN_DV), F32)],
        compiler_params=_params("arbitrary"),
        name="gdn_chunks",
    )(proj, proj, b_pre, a_pre, a_pre_t, conv_w,
      a_log.reshape(1, nh), dt_bias.reshape(1, nh), a_log.reshape(nh, 1), dt_bias.reshape(nh, 1),
      norm_g.reshape(1, GDN_DV))


def _softcap(t):
    return GATE_SOFTCAP * jnp.tanh(t / GATE_SOFTCAP)


def _mlstm_kernel(p_ref, i_ref, f_ref, it_ref, ft_ref, bi_ref, bf_ref, bit_ref, bft_ref, ng_ref,
                  o_ref, c_ref, n_ref, m_ref):
    n = MLSTM_CHUNK
    nh, dqk, dv = MLSTM_HEADS, MLSTM_DQK, MLSTM_DV

    @pl.when(pl.program_id(0) == 0)
    def _():
        c_ref[...] = jnp.zeros_like(c_ref)
        n_ref[...] = jnp.zeros_like(n_ref)
        m_ref[...] = jnp.zeros_like(m_ref)

    i_col = _softcap(i_ref[...] + bi_ref[...])
    i_row = _softcap(it_ref[...] + bit_ref[...])
    f_col = -_softplus(-_softcap(f_ref[...] + bf_ref[...]))
    f_row = -_softplus(-_softcap(ft_ref[...] + bft_ref[...]))
    b_col, b_row = _cumsum_both(f_col, f_row, n)
    incl, _ = _tri_masks(n)

    for h in range(nh):
        q = p_ref[:, h * dqk:(h + 1) * dqk] * dqk ** -0.5
        k = p_ref[:, nh * dqk + h * dqk:nh * dqk + (h + 1) * dqk]
        v = p_ref[:, 2 * nh * dqk + h * dv:2 * nh * dqk + (h + 1) * dv]
        bcc, bcr = b_col[:, h:h + 1], b_row[h:h + 1, :]
        icc, icr = i_col[:, h:h + 1], i_row[h:h + 1, :]
        m_prev = m_ref[h:h + 1, 0:1]
        d_log = jnp.where(incl, bcc - bcr + icr, -jnp.inf)
        d_max = jnp.max(d_log, axis=-1, keepdims=True)
        inter = bcc + m_prev
        m_t = jnp.maximum(inter, d_max)
        sc = _mxu(q, k, NT_DIMS) * jnp.exp(d_log - m_t)
        w_inter = jnp.exp(inter - m_t)
        c_mem = c_ref[h]
        n_mem = n_ref[h:h + 1, :]
        num = w_inter * _mxu(q, c_mem) + _mxu(sc, v)
        den = w_inter * jnp.sum(q * n_mem, axis=-1, keepdims=True) + jnp.sum(sc, axis=-1, keepdims=True)
        hid = num / jnp.maximum(jnp.abs(den), jnp.exp(-m_t))
        b_last = bcr[:, n - 1:n]
        end_max = jnp.max(b_last - bcr + icr, axis=-1, keepdims=True)
        m_new = jnp.maximum(b_last + m_prev, end_max)
        decay = jnp.exp(b_last + m_prev - m_new)
        k_w = k * jnp.exp(b_last - bcc + icc - m_new)
        c_ref[h] = decay * c_mem + _mxu(k_w, v, TN_DIMS)
        n_ref[h:h + 1, :] = decay * n_mem + jnp.sum(k_w, axis=0, keepdims=True)
        m_ref[h:h + 1, :] = jnp.broadcast_to(m_new, (1, m_ref.shape[1]))
        o_pre = p_ref[:, 2 * nh * dqk + nh * dv + h * dv:2 * nh * dqk + nh * dv + (h + 1) * dv]
        hid = _rms(hid, ng_ref[:, h * dv:(h + 1) * dv])
        o_ref[:, h * dv:(h + 1) * dv] = (hid * _sigmoid(o_pre)).astype(BF16)


def _mlstm_mixer(hn, w_in, j, b_i, b_f, norm_g):
    s = hn.shape[0]
    n = min(MLSTM_CHUNK, s)
    nh, dqk, dv = MLSTM_HEADS, MLSTM_DQK, MLSTM_DV
    c_main = 2 * nh * dqk + 2 * nh * dv
    proj = _mm_cols(hn, w_in, j, c_main)
    tail = _mm_tail(hn, w_in[j, :, c_main:])
    i_pre, f_pre = tail[:, :nh], tail[:, nh:]
    full = lambda shape: pl.BlockSpec(shape, lambda i: (0, 0))
    col = pl.BlockSpec((n, nh), lambda i: (i, 0))
    row = pl.BlockSpec((None, nh, n), lambda i: (i, 0, 0))
    return pl.pallas_call(
        _mlstm_kernel,
        grid=(s // n,),
        in_specs=[pl.BlockSpec((n, c_main), lambda i: (i, 0)), col, col, row, row,
                  full((1, nh)), full((1, nh)), full((nh, 1)), full((nh, 1)), full((1, nh * dv))],
        out_specs=pl.BlockSpec((n, nh * dv), lambda i: (i, 0)),
        out_shape=jax.ShapeDtypeStruct((s, nh * dv), BF16),
        scratch_shapes=[pltpu.VMEM((nh, dqk, dv), F32),
                        pltpu.VMEM((nh, dqk), F32),
                        pltpu.VMEM((nh, 128), F32)],
        compiler_params=_params("arbitrary"),
        name="mlstm_chunks",
    )(proj, i_pre, f_pre, _chunk_rows(i_pre, n), _chunk_rows(f_pre, n),
      b_i.reshape(1, nh), b_f.reshape(1, nh), b_i.reshape(nh, 1), b_f.reshape(nh, 1),
      norm_g.reshape(1, nh * dv))


def _ret_kernel(p_ref, pos_ref, fr_ref, ng_ref, o_ref, r_ref):
    n = RET_CHUNK
    nh, dk, dv = RET_HEADS, RET_DK, RET_DV
    half = dk // 2

    @pl.when(pl.program_id(0) == 0)
    def _():
        r_ref[...] = jnp.zeros_like(r_ref)

    ang = pos_ref[...].astype(F32) * fr_ref[...]
    cos, sin = jnp.cos(ang), jnp.sin(ang)
    row = lax.broadcasted_iota(jnp.int32, (n, n), 0)
    col = lax.broadcasted_iota(jnp.int32, (n, n), 1)
    lag = (row - col).astype(F32)
    t_col = lax.broadcasted_iota(jnp.int32, (n, 1), 0).astype(F32)

    def rope(t):
        t1, t2 = t[:, :half], t[:, half:]
        return jnp.concatenate([t1 * cos - t2 * sin, t1 * sin + t2 * cos], axis=1)

    for h in range(nh):
        log_gamma = float(np.log1p(-np.exp2(np.float32(-5.0 - h))))
        q = rope(p_ref[:, h * dk:(h + 1) * dk])
        k = rope(p_ref[:, nh * dk + h * dk:nh * dk + (h + 1) * dk]) * dk ** -0.5
        v = p_ref[:, 2 * nh * dk + h * dv:2 * nh * dk + (h + 1) * dv]
        d_mat = jnp.where(row >= col, jnp.exp(lag * log_gamma), 0.0)
        xi = jnp.exp((t_col + 1.0) * log_gamma)
        zeta = jnp.exp((n - 1.0 - t_col) * log_gamma)
        mem = r_ref[h]
        o = _mxu(_mxu(q, k, NT_DIMS) * d_mat, v) + _mxu(q, mem) * xi
        r_ref[h] = mem * math.exp(n * log_gamma) + _mxu(k * zeta, v, TN_DIMS)
        gate = p_ref[:, 2 * nh * dk + nh * dv + h * dv:2 * nh * dk + nh * dv + (h + 1) * dv]
        o = _rms(o, ng_ref[:, h * dv:(h + 1) * dv])
        o_ref[:, h * dv:(h + 1) * dv] = (o * _silu(gate)).astype(BF16)


def _ret_mixer(hn, positions, w_in, j, norm_g):
    s = hn.shape[0]
    n = min(RET_CHUNK, s)
    nh, dk, dv = RET_HEADS, RET_DK, RET_DV
    c_all = 2 * nh * dk + 2 * nh * dv
    proj = _mm_cols(hn, w_in, j, c_all)
    freqs = ROPE_BASE ** (-jnp.arange(0, dk, 2, dtype=F32) / dk)
    return pl.pallas_call(
        _ret_kernel,
        grid=(s // n,),
        in_specs=[pl.BlockSpec((n, c_all), lambda i: (i, 0)),
                  pl.BlockSpec((n, 1), lambda i: (i, 0)),
                  pl.BlockSpec((1, dk // 2), lambda i: (0, 0)),
                  pl.BlockSpec((1, nh * dv), lambda i: (0, 0))],
        out_specs=pl.BlockSpec((n, nh * dv), lambda i: (i, 0)),
        out_shape=jax.ShapeDtypeStruct((s, nh * dv), BF16),
        scratch_shapes=[pltpu.VMEM((nh, dk, dv), F32)],
        compiler_params=_params("arbitrary"),
        name="ret_chunks",
    )(proj, positions.reshape(s, 1), freqs.reshape(1, dk // 2), norm_g.reshape(1, nh * dv))


def _gelu_tanh(x):
    return 0.5 * x * (1.0 + jnp.tanh(math.sqrt(2.0 / math.pi) * (x + 0.044715 * (x * x * x))))


def _lru_kernel(p_ref, cw_ref, cb_ref, wr_ref, br_ref, wi_ref, bi_ref, lam_ref, o_ref, halo_ref, h_ref):
    rows = p_ref.shape[0]
    first = pl.program_id(0) == 0

    @pl.when(first)
    def _():
        h_ref[...] = jnp.zeros_like(h_ref)

    xb = _causal_conv(p_ref[:, :LRU_WIDTH], halo_ref, cw_ref[...], first) + cb_ref[...]
    neg_c_softplus = -LRU_C * _softplus(-lam_ref[...])
    t_idx = lax.broadcasted_iota(jnp.int32, (rows, LRU_BLOCK), 0)

    for b in range(LRU_BLOCKS):
        lanes = slice(b * LRU_BLOCK, (b + 1) * LRU_BLOCK)
        xblk = xb[:, lanes]
        r = _sigmoid(_mxu(xblk, wr_ref[b]) + br_ref[:, lanes])
        gi = _sigmoid(_mxu(xblk, wi_ref[b]) + bi_ref[:, lanes])
        log_a = r * neg_c_softplus[:, lanes]
        a = jnp.exp(log_a)
        u = jnp.sqrt(1.0 - jnp.exp(2.0 * log_a)) * (gi * xblk)
        d = 1
        while d < rows:
            keep = t_idx >= d
            u = u + a * jnp.where(keep, pltpu.roll(u, d, axis=0), 0.0)
            a = a * jnp.where(keep, pltpu.roll(a, d, axis=0), 1.0)
            d *= 2
        hs = u + a * h_ref[0:1, lanes]
        h_ref[0:1, lanes] = hs[rows - 1:rows]
        gate = p_ref[:, LRU_WIDTH + b * LRU_BLOCK:LRU_WIDTH + (b + 1) * LRU_BLOCK]
        o_ref[:, lanes] = (hs * _gelu_tanh(gate)).astype(BF16)


def _lru_mixer(hn, w_in, j, conv_w, conv_b, w_r, b_r, w_i, b_i, lam):
    s = hn.shape[0]
    rows = min(LRU_ROWS, s)
    w = LRU_WIDTH
    proj = _mm_cols(hn, w_in, j, 2 * w)
    vec = pl.BlockSpec((1, w), lambda i: (0, 0))
    blk = pl.BlockSpec((LRU_BLOCKS, LRU_BLOCK, LRU_BLOCK), lambda i: (0, 0, 0))
    return pl.pallas_call(
        _lru_kernel,
        grid=(s // rows,),
        in_specs=[pl.BlockSpec((rows, 2 * w), lambda i: (i, 0)),
                  pl.BlockSpec((CONV_WIDTH, w), lambda i: (0, 0)), vec, blk, vec, blk, vec, vec],
        out_specs=pl.BlockSpec((rows, w), lambda i: (i, 0)),
        out_shape=jax.ShapeDtypeStruct((s, w), BF16),
        scratch_shapes=[pltpu.VMEM((HALO_ROWS, w), F32), pltpu.VMEM((8, w), F32)],
        compiler_params=_params("arbitrary"),
        name="lru_scan",
    )(proj, conv_w, conv_b.reshape(1, w), w_r, b_r.reshape(1, w), w_i, b_i.reshape(1, w), lam.reshape(1, w))


@jax.jit
def kernel(x, positions, norm_g, mlp_w_up, mlp_w_down, gdn_w_in, gdn_conv_w, gdn_a_log, gdn_dt_bias, gdn_norm_g, gdn_w_out, mlstm_w_in, mlstm_b_i, mlstm_b_f, mlstm_norm_g, mlstm_w_out, ret_w_in, ret_norm_g, ret_w_out, lru_w_in, lru_conv_w, lru_conv_b, lru_w_r, lru_b_r, lru_w_i, lru_b_i, lru_lambda, lru_w_out):
    batch, seq, _ = x.shape
    assert batch == 1
    depth = norm_g.shape[0]
    xs = x.reshape(seq, D_MODEL)
    hn = _prenorm(xs, norm_g[0, 0])
    w_down = _to_bf16(mlp_w_down)
    w_outs = [_to_bf16(w) for w in (gdn_w_out, mlstm_w_out, ret_w_out, lru_w_out)]
    for layer in range(depth):
        kind, j = layer % 4, layer // 4
        if kind == 0:
            mixed = _gdn_mixer(hn, gdn_w_in, j, gdn_conv_w[j], gdn_a_log[j], gdn_dt_bias[j], gdn_norm_g[j])
        elif kind == 1:
            mixed = _mlstm_mixer(hn, mlstm_w_in, j, mlstm_b_i[j], mlstm_b_f[j], mlstm_norm_g[j])
        elif kind == 2:
            mixed = _ret_mixer(hn, positions, ret_w_in, j, ret_norm_g[j])
        else:
            mixed = _lru_mixer(hn, lru_w_in, j, lru_conv_w[j], lru_conv_b[j], lru_w_r[j], lru_b_r[j],
                               lru_w_i[j], lru_b_i[j], lru_lambda[j])
        xs, hn = _mm_rows(mixed, w_outs[kind], j, xs, norm_g[layer, 1], norm_g[layer, 2])
        hmid = _mm_cols(hn, mlp_w_up, layer, D_FF, act="relu2", out_dtype=BF16)
        g_next = norm_g[layer + 1, 0] if layer + 1 < depth else None
        xs, hn = _mm_rows(hmid, w_down, layer, xs, norm_g[layer, 3], g_next)
    return xs.reshape(batch, seq, D_MODEL)
```

```python
import functools
import math

import numpy as np
import jax
import jax.numpy as jnp
from jax import lax
from jax.experimental import pallas as pl
from jax.experimental.pallas import tpu as pltpu

F32 = jnp.float32
BF16 = jnp.bfloat16

D_MODEL = 2048
D_FF = 4 * D_MODEL
NORM_EPS = 1e-6
CONV_WIDTH = 4
HALO_ROWS = 8

GDN_HEADS, GDN_DK, GDN_DV = 16, 128, 128
GDN_CHUNK = 64
MLSTM_HEADS, MLSTM_DQK, MLSTM_DV = 8, 128, 256
MLSTM_CHUNK = 128
GATE_SOFTCAP = 15.0
RET_HEADS, RET_DK, RET_DV = 8, 256, 512
RET_CHUNK = 128
ROPE_BASE = 10000.0
LRU_WIDTH, LRU_BLOCKS, LRU_BLOCK = 2048, 16, 128
LRU_ROWS = 128
LRU_C = 8.0

VMEM_LIMIT_BYTES = 56 * 1024 * 1024

NT_DIMS = (((1,), (1,)), ((), ()))
TN_DIMS = (((0,), (0,)), ((), ()))


def _params(*semantics):
    return pltpu.CompilerParams(dimension_semantics=semantics,
                                vmem_limit_bytes=VMEM_LIMIT_BYTES)


def _rms(x, g):
    y = x * lax.rsqrt(jnp.mean(x * x, axis=-1, keepdims=True) + NORM_EPS)
    return y * g


def _sigmoid(x):
    return 1.0 / (1.0 + jnp.exp(-x))


def _silu(x):
    return x * _sigmoid(x)


def _softplus(x):
    return jnp.maximum(x, 0.0) + jnp.log1p(jnp.exp(-jnp.abs(x)))


def _mxu(a, b, dims=None):
    a = a.astype(BF16)
    b = b.astype(BF16)
    if dims is None:
        return jnp.dot(a, b, preferred_element_type=F32)
    return lax.dot_general(a, b, dims, preferred_element_type=F32)


def _split_bf16(a):
    hi = a.astype(BF16)
    lo = (a - hi.astype(F32)).astype(BF16)
    return hi, lo


def _mxu3(a, b):
    m = a.shape[0]
    a_hi, a_lo = _split_bf16(a)
    b_hi, b_lo = _split_bf16(b)
    top = jnp.dot(jnp.concatenate([a_hi, a_lo], axis=0), b_hi, preferred_element_type=F32)
    return top[:m] + top[m:] + jnp.dot(a_hi, b_lo, preferred_element_type=F32)


def _tri_masks(n):
    row = lax.broadcasted_iota(jnp.int32, (n, n), 0)
    col = lax.broadcasted_iota(jnp.int32, (n, n), 1)
    return row >= col, row > col


def _cumsum_both(col_vals, row_vals, n):
    row = lax.broadcasted_iota(jnp.int32, (n, n), 0)
    col = lax.broadcasted_iota(jnp.int32, (n, n), 1)
    lower = jnp.where(row >= col, 1.0, 0.0).astype(F32)
    upper = jnp.where(row <= col, 1.0, 0.0).astype(F32)
    return _mxu3(lower, col_vals), _mxu3(row_vals, upper)


def _prenorm_kernel(x_ref, g_ref, o_ref):
    o_ref[...] = _rms(x_ref[...], g_ref[...]).astype(o_ref.dtype)


def _prenorm(x, g):
    s = x.shape[0]
    tm = 512
    return pl.pallas_call(
        _prenorm_kernel,
        grid=(s // tm,),
        in_specs=[pl.BlockSpec((tm, D_MODEL), lambda i: (i, 0)),
                  pl.BlockSpec((1, D_MODEL), lambda i: (0, 0))],
        out_specs=pl.BlockSpec((tm, D_MODEL), lambda i: (i, 0)),
        out_shape=jax.ShapeDtypeStruct((s, D_MODEL), BF16),
        compiler_params=_params("parallel"),
        name="prenorm",
    )(x, g.reshape(1, D_MODEL))


def _mm_cols_kernel(x_ref, w_ref, o_ref, wb_ref, *, act):
    @pl.when(pl.program_id(1) == 0)
    def _():
        wb_ref[...] = w_ref[...].astype(BF16)

    acc = jnp.dot(x_ref[...], wb_ref[...], preferred_element_type=F32)
    if act == "relu2":
        acc = jnp.square(jnp.maximum(acc, 0.0))
    o_ref[...] = acc.astype(o_ref.dtype)


def _mm_cols(xb, w, layer, n_cols, act=None, out_dtype=F32, tm=1024, tn=1024):
    m, k = xb.shape
    tm = min(tm, m)
    return pl.pallas_call(
        functools.partial(_mm_cols_kernel, act=act),
        grid=(n_cols // tn, m // tm),
        in_specs=[pl.BlockSpec((tm, k), lambda j, i: (i, 0)),
                  pl.BlockSpec((None, k, tn), lambda j, i: (layer, 0, j))],
        out_specs=pl.BlockSpec((tm, tn), lambda j, i: (i, j)),
        out_shape=jax.ShapeDtypeStruct((m, n_cols), out_dtype),
        scratch_shapes=[pltpu.VMEM((k, tn), BF16)],
        compiler_params=_params("arbitrary", "arbitrary"),
        name="mm_cols",
    )(xb, w)


def _tail_kernel(x_ref, w_ref, o_ref):
    o_ref[...] = jnp.dot(x_ref[...], w_ref[...].astype(BF16), preferred_element_type=F32)


def _mm_tail(xb, w_tail):
    m, k = xb.shape
    nt = w_tail.shape[1]
    tm = min(512, m)
    return pl.pallas_call(
        _tail_kernel,
        grid=(m // tm,),
        in_specs=[pl.BlockSpec((tm, k), lambda i: (i, 0)),
                  pl.BlockSpec((k, nt), lambda i: (0, 0))],
        out_specs=pl.BlockSpec((tm, nt), lambda i: (i, 0)),
        out_shape=jax.ShapeDtypeStruct((m, nt), F32),
        compiler_params=_params("parallel"),
        name="mm_tail",
    )(xb, w_tail)


def _chunk_rows(col_vals, n):
    s, nh = col_vals.shape
    return jnp.swapaxes(col_vals.reshape(s // n, n, nh), 1, 2)


def _cast_kernel(w_ref, o_ref):
    o_ref[...] = w_ref[...].astype(o_ref.dtype)


def _to_bf16(w, tk=512):
    nl, k, n = w.shape
    spec = pl.BlockSpec((None, tk, n), lambda l, kk: (l, kk, 0))
    return pl.pallas_call(
        _cast_kernel,
        grid=(nl, k // tk),
        in_specs=[spec],
        out_specs=spec,
        out_shape=jax.ShapeDtypeStruct(w.shape, BF16),
        compiler_params=_params("parallel", "parallel"),
        name="to_bf16",
    )(w)


def _mm_rows_kernel(a_ref, w_ref, x_ref, gp_ref, gn_ref, *refs, nk, with_next):
    xo_ref = refs[0]

    def finish(acc):
        x_new = x_ref[...] + _rms(acc, gp_ref[...])
        xo_ref[...] = x_new
        if with_next:
            refs[1][...] = _rms(x_new, gn_ref[...]).astype(BF16)

    if nk == 1:
        finish(jnp.dot(a_ref[...], w_ref[...], preferred_element_type=F32))
        return

    kk = pl.program_id(1)

    @pl.when(kk == 0)
    def _():
        xo_ref[...] = jnp.zeros_like(xo_ref)

    xo_ref[...] += jnp.dot(a_ref[...], w_ref[...], preferred_element_type=F32)

    @pl.when(kk == nk - 1)
    def _():
        finish(xo_ref[...])


def _mm_rows_tiles(m, k):
    if k <= D_MODEL:
        return min(512, m), k
    return min(1024, m), 1024


def _mm_rows(a, w, layer, x, g_post, g_next):
    m, k = a.shape
    tm, tk = _mm_rows_tiles(m, k)
    nk = k // tk
    with_next = g_next is not None
    if g_next is None:
        g_next = g_post
    row_spec = pl.BlockSpec((tm, D_MODEL), lambda i, kk: (i, 0))
    res_spec = row_spec if nk == 1 else pl.BlockSpec((tm, D_MODEL), lambda i, kk: (i, 0),
                                                      pipeline_mode=pl.Buffered(1))
    vec_spec = pl.BlockSpec((1, D_MODEL), lambda i, kk: (0, 0))
    out_specs = [row_spec]
    out_shape = [jax.ShapeDtypeStruct((m, D_MODEL), F32)]
    if with_next:
        out_specs.append(row_spec)
        out_shape.append(jax.ShapeDtypeStruct((m, D_MODEL), BF16))
    res = pl.pallas_call(
        functools.partial(_mm_rows_kernel, nk=nk, with_next=with_next),
        grid=(m // tm, nk),
        in_specs=[pl.BlockSpec((tm, tk), lambda i, kk: (i, kk)),
                  pl.BlockSpec((None, tk, D_MODEL), lambda i, kk: (layer, kk, 0)),
                  res_spec, vec_spec, vec_spec],
        out_specs=out_specs,
        out_shape=out_shape,
        compiler_params=_params("parallel", "arbitrary"),
        name="mm_rows",
    )(a, w, x, g_post.reshape(1, D_MODEL), g_next.reshape(1, D_MODEL))
    return (res[0], res[1]) if with_next else (res[0], None)


def _causal_conv(x, halo_ref, cw, first):
    rows = x.shape[0]

    @pl.when(first)
    def _():
        halo_ref[...] = jnp.zeros_like(halo_ref)

    xx = jnp.concatenate([halo_ref[...], x], axis=0)
    y = cw[0:1] * xx[HALO_ROWS - 3:HALO_ROWS - 3 + rows]
    y = y + cw[1:2] * xx[HALO_ROWS - 2:HALO_ROWS - 2 + rows]
    y = y + cw[2:3] * xx[HALO_ROWS - 1:HALO_ROWS - 1 + rows]
    y = y + cw[3:4] * x
    halo_ref[...] = x[rows - HALO_ROWS:]
    return y


GDN_HEAD_GROUP = 16


def _unit_lower_inverses(mats):
    n = mats[0].shape[0]
    row = lax.broadcasted_iota(jnp.int32, (n, n), 0)
    col = lax.broadcasted_iota(jnp.int32, (n, n), 1)
    eye = jnp.where(row == col, 1.0, 0.0).astype(F32)

    def same_block(b):
        shift = int(math.log2(b))
        return (row >> shift) == (col >> shift)

    b = 2
    ds = [eye - jnp.where(same_block(b), a, 0.0) for a in mats]
    while b < n:
        inner, outer = same_block(b), same_block(2 * b)
        es = [jnp.where(outer, jnp.where(inner, 0.0, a), 0.0) for a in mats]
        eds = [_mxu(e, d) for e, d in zip(es, ds)]
        ds = [d - _mxu(d, ed) for d, ed in zip(ds, eds)]
        b *= 2
    return ds


def _gdn_kernel(qkv_ref, z_ref, b_ref, a_ref, at_ref, cw_ref, al_ref, dtb_ref, alt_ref, dtbt_ref, ng_ref,
                o_ref, halo_ref, act_ref, s_ref):
    n = GDN_CHUNK
    hdk = GDN_HEADS * GDN_DK
    first = pl.program_id(0) == 0

    @pl.when(first)
    def _():
        s_ref[...] = jnp.zeros_like(s_ref)

    y = _causal_conv(qkv_ref[...], halo_ref, cw_ref[...], first)
    act_ref[...] = _silu(y)

    beta = _sigmoid(b_ref[...])
    g_col = -jnp.exp(al_ref[...]) * _softplus(a_ref[...] + dtb_ref[...])
    g_row = -jnp.exp(alt_ref[...]) * _softplus(at_ref[...] + dtbt_ref[...])
    gc_col, gc_row = _cumsum_both(g_col, g_row, n)
    incl, strict = _tri_masks(n)
    ng = ng_ref[...]

    def l2norm(t):
        return t * lax.rsqrt(jnp.sum(t * t, axis=-1, keepdims=True) + NORM_EPS)

    for g0 in range(0, GDN_HEADS, GDN_HEAD_GROUP):
        heads = range(g0, g0 + GDN_HEAD_GROUP)
        q = [l2norm(act_ref[:, h * GDN_DK:(h + 1) * GDN_DK]) * GDN_DK ** -0.5 for h in heads]
        k = [l2norm(act_ref[:, hdk + h * GDN_DK:hdk + (h + 1) * GDN_DK]) for h in heads]
        v = [act_ref[:, 2 * hdk + h * GDN_DV:2 * hdk + (h + 1) * GDN_DV] for h in heads]
        gcc = [gc_col[:, h:h + 1] for h in heads]
        gcr = [gc_row[h:h + 1, :] for h in heads]
        bc = [beta[:, h:h + 1] for h in heads]
        dec = [jnp.where(incl, jnp.exp(c - r), 0.0) for c, r in zip(gcc, gcr)]
        prod = [_mxu(jnp.concatenate([ki, qi], axis=0), ki, NT_DIMS) for ki, qi in zip(k, q)]
        a_mat = [jnp.where(strict, b * p[:n] * d, 0.0) for b, p, d in zip(bc, prod, dec)]
        t_inv = _unit_lower_inverses(a_mat)
        eg = [jnp.exp(c) for c in gcc]
        rhs = [jnp.concatenate([vi * b, ki * (b * e)], axis=1) for vi, ki, b, e in zip(v, k, bc, eg)]
        uw = [_mxu3(t, r) for t, r in zip(t_inv, rhs)]
        state = [s_ref[h] for h in heads]
        qs_ws = [_mxu(jnp.concatenate([qi * e, x[:, GDN_DV:]], axis=0), st)
                 for qi, e, x, st in zip(q, eg, uw, state)]
        v_new = [x[:, :GDN_DV] - y[n:] for x, y in zip(uw, qs_ws)]
        o = [y[:n] + _mxu(p[n:] * d, vn) for y, p, d, vn in zip(qs_ws, prod, dec, v_new)]
        g_last = [r[:, n - 1:n] for r in gcr]
        upd = [_mxu(ki * jnp.exp(gl - c), vn, TN_DIMS) for ki, gl, c, vn in zip(k, g_last, gcc, v_new)]
        for i, h in enumerate(heads):
            s_ref[h] = state[i] * jnp.exp(g_last[i]) + upd[i]
            zh = z_ref[:, h * GDN_DV:(h + 1) * GDN_DV]
            o_ref[:, h * GDN_DV:(h + 1) * GDN_DV] = (_rms(o[i], ng) * _silu(zh)).astype(BF16)


def _gdn_mixer(hn, w_in, j, conv_w, a_log, dt_bias, norm_g):
    s = hn.shape[0]
    n = GDN_CHUNK
    nh = GDN_HEADS
    c_qkv = 2 * nh * GDN_DK + nh * GDN_DV
    c_z = nh * GDN_DV
    proj = _mm_cols(hn, w_in, j, c_qkv + c_z)
    tail = _mm_tail(hn, w_in[j, :, c_qkv + c_z:])
    b_pre, a_pre = tail[:, :nh], tail[:, nh:]
    a_pre_t = _chunk_rows(a_pre, n)
    full = lambda shape: pl.BlockSpec(shape, lambda i: (0, 0))
    return pl.pallas_call(
        _gdn_kernel,
        grid=(s // n,),
        in_specs=[pl.BlockSpec((n, c_qkv), lambda i: (i, 0)),
                  pl.BlockSpec((n, c_z), lambda i: (i, c_qkv // c_z)),
                  pl.BlockSpec((n, nh), lambda i: (i, 0)),
                  pl.BlockSpec((n, nh), lambda i: (i, 0)),
                  pl.BlockSpec((None, nh, n), lambda i: (i, 0, 0)),
                  full((CONV_WIDTH, c_qkv)),
                  full((1, nh)), full((1, nh)), full((nh, 1)), full((nh, 1)),
                  full((1, GDN_DV))],
        out_specs=pl.BlockSpec((n, c_z), lambda i: (i, 0)),
        out_shape=jax.ShapeDtypeStruct((s, c_z), BF16),
        scratch_shapes=[pltpu.VMEM((HALO_ROWS, c_qkv), F32),
                        pltpu.VMEM((n, c_qkv), F32),
                        pltpu.VMEM((nh, GDN_DK, GDN_DV), F32)],
        compiler_params=_params("arbitrary"),
        name="gdn_chunks",
    )(proj, proj, b_pre, a_pre, a_pre_t, conv_w,
      a_log.reshape(1, nh), dt_bias.reshape(1, nh), a_log.reshape(nh, 1), dt_bias.reshape(nh, 1),
      norm_g.reshape(1, GDN_DV))


def _softcap(t):
    return GATE_SOFTCAP * jnp.tanh(t / GATE_SOFTCAP)


def _mlstm_kernel(p_ref, i_ref, f_ref, it_ref, ft_ref, bi_ref, bf_ref, bit_ref, bft_ref, ng_ref,
                  o_ref, c_ref, n_ref, m_ref):
    n = MLSTM_CHUNK
    nh, dqk, dv = MLSTM_HEADS, MLSTM_DQK, MLSTM_DV

    @pl.when(pl.program_id(0) == 0)
    def _():
        c_ref[...] = jnp.zeros_like(c_ref)
        n_ref[...] = jnp.zeros_like(n_ref)
        m_ref[...] = jnp.zeros_like(m_ref)

    i_col = _softcap(i_ref[...] + bi_ref[...])
    i_row = _softcap(it_ref[...] + bit_ref[...])
    f_col = -_softplus(-_softcap(f_ref[...] + bf_ref[...]))
    f_row = -_softplus(-_softcap(ft_ref[...] + bft_ref[...]))
    b_col, b_row = _cumsum_both(f_col, f_row, n)
    incl, _ = _tri_masks(n)

    heads = range(nh)
    q = [p_ref[:, h * dqk:(h + 1) * dqk] * dqk ** -0.5 for h in heads]
    k = [p_ref[:, nh * dqk + h * dqk:nh * dqk + (h + 1) * dqk] for h in heads]
    v = [p_ref[:, 2 * nh * dqk + h * dv:2 * nh * dqk + (h + 1) * dv] for h in heads]
    bcc = [b_col[:, h:h + 1] for h in heads]
    bcr = [b_row[h:h + 1, :] for h in heads]
    icc = [i_col[:, h:h + 1] for h in heads]
    icr = [i_row[h:h + 1, :] for h in heads]
    m_prev = [m_ref[h:h + 1, 0:1] for h in heads]
    qk = [_mxu(qi, ki, NT_DIMS) for qi, ki in zip(q, k)]
    c_mem = [c_ref[h] for h in heads]
    n_mem = [n_ref[h:h + 1, :] for h in heads]
    qc = [_mxu(qi, c) for qi, c in zip(q, c_mem)]
    d_log = [jnp.where(incl, c - r + i, -jnp.inf) for c, r, i in zip(bcc, bcr, icr)]
    inter = [c + m for c, m in zip(bcc, m_prev)]
    m_t = [jnp.maximum(x, jnp.max(d, axis=-1, keepdims=True)) for x, d in zip(inter, d_log)]
    sc = [s * jnp.exp(d - m) for s, d, m in zip(qk, d_log, m_t)]
    w_inter = [jnp.exp(x - m) for x, m in zip(inter, m_t)]
    num = [w * x + _mxu(s, vi) for w, x, s, vi in zip(w_inter, qc, sc, v)]
    den = [w * jnp.sum(qi * nm, axis=-1, keepdims=True) + jnp.sum(s, axis=-1, keepdims=True)
           for w, qi, nm, s in zip(w_inter, q, n_mem, sc)]
    hid = [x / jnp.maximum(jnp.abs(d), jnp.exp(-m)) for x, d, m in zip(num, den, m_t)]
    b_last = [r[:, n - 1:n] for r in bcr]
    end_max = [jnp.max(bl - r + i, axis=-1, keepdims=True) for bl, r, i in zip(b_last, bcr, icr)]
    m_new = [jnp.maximum(bl + m, e) for bl, m, e in zip(b_last, m_prev, end_max)]
    decay = [jnp.exp(bl + m - mn) for bl, m, mn in zip(b_last, m_prev, m_new)]
    k_w = [ki * jnp.exp(bl - c + i - mn) for ki, bl, c, i, mn in zip(k, b_last, bcc, icc, m_new)]
    upd = [_mxu(kw, vi, TN_DIMS) for kw, vi in zip(k_w, v)]
    for h in heads:
        c_ref[h] = decay[h] * c_mem[h] + upd[h]
        n_ref[h:h + 1, :] = decay[h] * n_mem[h] + jnp.sum(k_w[h], axis=0, keepdims=True)
        m_ref[h:h + 1, :] = jnp.broadcast_to(m_new[h], (1, m_ref.shape[1]))
        o_pre = p_ref[:, 2 * nh * dqk + nh * dv + h * dv:2 * nh * dqk + nh * dv + (h + 1) * dv]
        normed = _rms(hid[h], ng_ref[:, h * dv:(h + 1) * dv])
        o_ref[:, h * dv:(h + 1) * dv] = (normed * _sigmoid(o_pre)).astype(BF16)


def _mlstm_mixer(hn, w_in, j, b_i, b_f, norm_g):
    s = hn.shape[0]
    n = min(MLSTM_CHUNK, s)
    nh, dqk, dv = MLSTM_HEADS, MLSTM_DQK, MLSTM_DV
    c_main = 2 * nh * dqk + 2 * nh * dv
    proj = _mm_cols(hn, w_in, j, c_main)
    tail = _mm_tail(hn, w_in[j, :, c_main:])
    i_pre, f_pre = tail[:, :nh], tail[:, nh:]
    full = lambda shape: pl.BlockSpec(shape, lambda i: (0, 0))
    col = pl.BlockSpec((n, nh), lambda i: (i, 0))
    row = pl.BlockSpec((None, nh, n), lambda i: (i, 0, 0))
    return pl.pallas_call(
        _mlstm_kernel,
        grid=(s // n,),
        in_specs=[pl.BlockSpec((n, c_main), lambda i: (i, 0)), col, col, row, row,
                  full((1, nh)), full((1, nh)), full((nh, 1)), full((nh, 1)), full((1, nh * dv))],
        out_specs=pl.BlockSpec((n, nh * dv), lambda i: (i, 0)),
        out_shape=jax.ShapeDtypeStruct((s, nh * dv), BF16),
        scratch_shapes=[pltpu.VMEM((nh, dqk, dv), F32),
                        pltpu.VMEM((nh, dqk), F32),
                        pltpu.VMEM((nh, 128), F32)],
        compiler_params=_params("arbitrary"),
        name="mlstm_chunks",
    )(proj, i_pre, f_pre, _chunk_rows(i_pre, n), _chunk_rows(f_pre, n),
      b_i.reshape(1, nh), b_f.reshape(1, nh), b_i.reshape(nh, 1), b_f.reshape(nh, 1),
      norm_g.reshape(1, nh * dv))


def _ret_kernel(p_ref, pos_ref, fr_ref, ng_ref, o_ref, r_ref):
    n = RET_CHUNK
    nh, dk, dv = RET_HEADS, RET_DK, RET_DV
    half = dk // 2

    @pl.when(pl.program_id(0) == 0)
    def _():
        r_ref[...] = jnp.zeros_like(r_ref)

    ang = pos_ref[...].astype(F32) * fr_ref[...]
    cos, sin = jnp.cos(ang), jnp.sin(ang)
    row = lax.broadcasted_iota(jnp.int32, (n, n), 0)
    col = lax.broadcasted_iota(jnp.int32, (n, n), 1)
    lag = (row - col).astype(F32)
    t_col = lax.broadcasted_iota(jnp.int32, (n, 1), 0).astype(F32)

    def rope(t):
        t1, t2 = t[:, :half], t[:, half:]
        return jnp.concatenate([t1 * cos - t2 * sin, t1 * sin + t2 * cos], axis=1)

    for h in range(nh):
        log_gamma = float(np.log1p(-np.exp2(np.float32(-5.0 - h))))
        q = rope(p_ref[:, h * dk:(h + 1) * dk])
        k = rope(p_ref[:, nh * dk + h * dk:nh * dk + (h + 1) * dk]) * dk ** -0.5
        v = p_ref[:, 2 * nh * dk + h * dv:2 * nh * dk + (h + 1) * dv]
        d_mat = jnp.where(row >= col, jnp.exp(lag * log_gamma), 0.0)
        xi = jnp.exp((t_col + 1.0) * log_gamma)
        zeta = jnp.exp((n - 1.0 - t_col) * log_gamma)
        mem = r_ref[h]
        o = _mxu(_mxu(q, k, NT_DIMS) * d_mat, v) + _mxu(q, mem) * xi
        r_ref[h] = mem * math.exp(n * log_gamma) + _mxu(k * zeta, v, TN_DIMS)
        gate = p_ref[:, 2 * nh * dk + nh * dv + h * dv:2 * nh * dk + nh * dv + (h + 1) * dv]
        o = _rms(o, ng_ref[:, h * dv:(h + 1) * dv])
        o_ref[:, h * dv:(h + 1) * dv] = (o * _silu(gate)).astype(BF16)


def _ret_mixer(hn, positions, w_in, j, norm_g):
    s = hn.shape[0]
    n = min(RET_CHUNK, s)
    nh, dk, dv = RET_HEADS, RET_DK, RET_DV
    c_all = 2 * nh * dk + 2 * nh * dv
    proj = _mm_cols(hn, w_in, j, c_all)
    freqs = ROPE_BASE ** (-jnp.arange(0, dk, 2, dtype=F32) / dk)
    return pl.pallas_call(
        _ret_kernel,
        grid=(s // n,),
        in_specs=[pl.BlockSpec((n, c_all), lambda i: (i, 0)),
                  pl.BlockSpec((n, 1), lambda i: (i, 0)),
                  pl.BlockSpec((1, dk // 2), lambda i: (0, 0)),
                  pl.BlockSpec((1, nh * dv), lambda i: (0, 0))],
        out_specs=pl.BlockSpec((n, nh * dv), lambda i: (i, 0)),
        out_shape=jax.ShapeDtypeStruct((s, nh * dv), BF16),
        scratch_shapes=[pltpu.VMEM((nh, dk, dv), F32)],
        compiler_params=_params("arbitrary"),
        name="ret_chunks",
    )(proj, positions.reshape(s, 1), freqs.reshape(1, dk // 2), norm_g.reshape(1, nh * dv))


def _gelu_tanh(x):
    return 0.5 * x * (1.0 + jnp.tanh(math.sqrt(2.0 / math.pi) * (x + 0.044715 * (x * x * x))))


def _lru_kernel(p_ref, cw_ref, cb_ref, wr_ref, br_ref, wi_ref, bi_ref, lam_ref, o_ref,
                stage_ref, out_ref, halo_ref, h_ref):
    rows = p_ref.shape[0]
    seg = rows // 8
    nb = LRU_BLOCKS
    first = pl.program_id(0) == 0

    @pl.when(first)
    def _():
        h_ref[...] = jnp.zeros_like(h_ref)
        halo_ref[...] = jnp.zeros_like(halo_ref)

    for c in range(2 * nb):
        stage_ref[c] = p_ref[:, c * LRU_BLOCK:(c + 1) * LRU_BLOCK]

    sub = lax.broadcasted_iota(jnp.int32, (8, LRU_BLOCK), 0)
    neg_c_softplus = -LRU_C * _softplus(-lam_ref[...])
    taps = CONV_WIDTH - 1

    for b in range(nb):
        lanes = slice(b * LRU_BLOCK, (b + 1) * LRU_BLOCK)
        xv = [stage_ref[b, pl.ds(v, 8, stride=seg), :] for v in range(seg)]
        wrap = [jnp.where(sub == 0,
                          pltpu.roll(halo_ref[b, taps - d], 1, axis=0),
                          pltpu.roll(xv[seg - d], 1, axis=0)) for d in range(1, taps + 1)]
        for d in range(1, taps + 1):
            halo_ref[b, taps - d] = xv[seg - d]

        def src(v):
            return xv[v] if v >= 0 else wrap[-v - 1]

        cw = cw_ref[:, lanes]
        cb = cb_ref[:, lanes]
        xblk = jnp.concatenate(
            [cw[0:1] * src(v - 3) + cw[1:2] * src(v - 2) + cw[2:3] * src(v - 1) + cw[3:4] * src(v) + cb
             for v in range(seg)], axis=0)
        r = _sigmoid(_mxu(xblk, wr_ref[b]) + br_ref[:, lanes])
        gi = _sigmoid(_mxu(xblk, wi_ref[b]) + bi_ref[:, lanes])
        log_a = r * neg_c_softplus[:, lanes]
        a = jnp.exp(log_a)
        u = jnp.sqrt(1.0 - jnp.exp(2.0 * log_a)) * (gi * xblk)

        hs, ps = [u[0:8]], [a[0:8]]
        for v in range(1, seg):
            av = a[v * 8:(v + 1) * 8]
            hs.append(av * hs[-1] + u[v * 8:(v + 1) * 8])
            ps.append(av * ps[-1])
        e, pc = hs[-1], ps[-1]
        d = 1
        while d < 8:
            keep = sub >= d
            e = e + pc * jnp.where(keep, pltpu.roll(e, d, axis=0), 0.0)
            pc = pc * jnp.where(keep, pltpu.roll(pc, d, axis=0), 1.0)
            d *= 2
        h0 = h_ref[b, 0:1, :]
        end_state = e + pc * h0
        carry = jnp.where(sub == 0, h0, pltpu.roll(end_state, 1, axis=0))
        h_ref[b, 0:1, :] = end_state[7:8]
        for v in range(seg):
            gate = stage_ref[nb + b, pl.ds(v, 8, stride=seg), :]
            out_ref[b, pl.ds(v, 8, stride=seg), :] = (hs[v] + ps[v] * carry) * _gelu_tanh(gate)

    for b in range(nb):
        o_ref[:, b * LRU_BLOCK:(b + 1) * LRU_BLOCK] = out_ref[b].astype(BF16)


def _lru_mixer(hn, w_in, j, conv_w, conv_b, w_r, b_r, w_i, b_i, lam):
    s = hn.shape[0]
    rows = min(LRU_ROWS, s)
    w = LRU_WIDTH
    proj = _mm_cols(hn, w_in, j, 2 * w)
    vec = pl.BlockSpec((1, w), lambda i: (0, 0))
    blk = pl.BlockSpec((LRU_BLOCKS, LRU_BLOCK, LRU_BLOCK), lambda i: (0, 0, 0))
    return pl.pallas_call(
        _lru_kernel,
        grid=(s // rows,),
        in_specs=[pl.BlockSpec((rows, 2 * w), lambda i: (i, 0)),
                  pl.BlockSpec((CONV_WIDTH, w), lambda i: (0, 0)), vec, blk, vec, blk, vec, vec],
        out_specs=pl.BlockSpec((rows, w), lambda i: (i, 0)),
        out_shape=jax.ShapeDtypeStruct((s, w), BF16),
        scratch_shapes=[pltpu.VMEM((2 * LRU_BLOCKS, rows, LRU_BLOCK), F32),
                        pltpu.VMEM((LRU_BLOCKS, rows, LRU_BLOCK), F32),
                        pltpu.VMEM((LRU_BLOCKS, CONV_WIDTH - 1, 8, LRU_BLOCK), F32),
                        pltpu.VMEM((LRU_BLOCKS, 8, LRU_BLOCK), F32)],
        compiler_params=_params("arbitrary"),
        name="lru_scan",
    )(proj, conv_w, conv_b.reshape(1, w), w_r, b_r.reshape(1, w), w_i, b_i.reshape(1, w), lam.reshape(1, w))


@jax.jit
def kernel(x, positions, norm_g, mlp_w_up, mlp_w_down, gdn_w_in, gdn_conv_w, gdn_a_log, gdn_dt_bias, gdn_norm_g, gdn_w_out, mlstm_w_in, mlstm_b_i, mlstm_b_f, mlstm_norm_g, mlstm_w_out, ret_w_in, ret_norm_g, ret_w_out, lru_w_in, lru_conv_w, lru_conv_b, lru_w_r, lru_b_r, lru_w_i, lru_b_i, lru_lambda, lru_w_out):
    batch, seq, _ = x.shape
    assert batch == 1
    depth = norm_g.shape[0]
    xs = x.reshape(seq, D_MODEL)
    hn = _prenorm(xs, norm_g[0, 0])
    w_down = _to_bf16(mlp_w_down)
    w_outs = [_to_bf16(w) for w in (gdn_w_out, mlstm_w_out, ret_w_out, lru_w_out)]
    for layer in range(depth):
        kind, j = layer % 4, layer // 4
        if kind == 0:
            mixed = _gdn_mixer(hn, gdn_w_in, j, gdn_conv_w[j], gdn_a_log[j], gdn_dt_bias[j], gdn_norm_g[j])
        elif kind == 1:
            mixed = _mlstm_mixer(hn, mlstm_w_in, j, mlstm_b_i[j], mlstm_b_f[j], mlstm_norm_g[j])
        elif kind == 2:
            mixed = _ret_mixer(hn, positions, ret_w_in, j, ret_norm_g[j])
        else:
            mixed = _lru_mixer(hn, lru_w_in, j, lru_conv_w[j], lru_conv_b[j], lru_w_r[j], lru_b_r[j],
                               lru_w_i[j], lru_b_i[j], lru_lambda[j])
        xs, hn = _mm_rows(mixed, w_outs[kind], j, xs, norm_g[layer, 1], norm_g[layer, 2])
        hmid = _mm_cols(hn, mlp_w_up, layer, D_FF, act="relu2", out_dtype=BF16)
        g_next = norm_g[layer + 1, 0] if layer + 1 < depth else None
        xs, hn = _mm_rows(hmid, w_down, layer, xs, norm_g[layer, 3], g_next)
    return xs.reshape(batch, seq, D_MODEL)
```

```python
import functools
import math

import numpy as np
import jax
import jax.numpy as jnp
from jax import lax
from jax.experimental import pallas as pl
from jax.experimental.pallas import tpu as pltpu

F32 = jnp.float32
BF16 = jnp.bfloat16

D_MODEL = 2048
D_FF = 4 * D_MODEL
NORM_EPS = 1e-6
CONV_WIDTH = 4
HALO_ROWS = 8

GDN_HEADS, GDN_DK, GDN_DV = 16, 128, 128
GDN_CHUNK = 64
MLSTM_HEADS, MLSTM_DQK, MLSTM_DV = 8, 128, 256
MLSTM_CHUNK = 128
GATE_SOFTCAP = 15.0
RET_HEADS, RET_DK, RET_DV = 8, 256, 512
RET_CHUNK = 128
ROPE_BASE = 10000.0
LRU_WIDTH, LRU_BLOCKS, LRU_BLOCK = 2048, 16, 128
LRU_ROWS = 128
LRU_C = 8.0

VMEM_LIMIT_BYTES = 56 * 1024 * 1024

NT_DIMS = (((1,), (1,)), ((), ()))
TN_DIMS = (((0,), (0,)), ((), ()))


def _params(*semantics):
    return pltpu.CompilerParams(dimension_semantics=semantics,
                                vmem_limit_bytes=VMEM_LIMIT_BYTES)


def _rms(x, g):
    y = x * lax.rsqrt(jnp.mean(x * x, axis=-1, keepdims=True) + NORM_EPS)
    return y * g


def _sigmoid(x):
    return 1.0 / (1.0 + jnp.exp(-x))


def _silu(x):
    return x * _sigmoid(x)


def _softplus(x):
    return jnp.maximum(x, 0.0) + jnp.log1p(jnp.exp(-jnp.abs(x)))


def _mxu(a, b, dims=None):
    a = a.astype(BF16)
    b = b.astype(BF16)
    if dims is None:
        return jnp.dot(a, b, preferred_element_type=F32)
    return lax.dot_general(a, b, dims, preferred_element_type=F32)


def _split_bf16(a):
    hi = a.astype(BF16)
    lo = (a - hi.astype(F32)).astype(BF16)
    return hi, lo


def _mxu3(a, b):
    m = a.shape[0]
    a_hi, a_lo = _split_bf16(a)
    b_hi, b_lo = _split_bf16(b)
    top = jnp.dot(jnp.concatenate([a_hi, a_lo], axis=0), b_hi, preferred_element_type=F32)
    return top[:m] + top[m:] + jnp.dot(a_hi, b_lo, preferred_element_type=F32)


def _tri_masks(n):
    row = lax.broadcasted_iota(jnp.int32, (n, n), 0)
    col = lax.broadcasted_iota(jnp.int32, (n, n), 1)
    return row >= col, row > col


def _cumsum_both(col_vals, row_vals, n):
    row = lax.broadcasted_iota(jnp.int32, (n, n), 0)
    col = lax.broadcasted_iota(jnp.int32, (n, n), 1)
    lower = jnp.where(row >= col, 1.0, 0.0).astype(F32)
    upper = jnp.where(row <= col, 1.0, 0.0).astype(F32)
    return _mxu3(lower, col_vals), _mxu3(row_vals, upper)


def _prenorm_kernel(x_ref, g_ref, o_ref):
    o_ref[...] = _rms(x_ref[...], g_ref[...]).astype(o_ref.dtype)


def _prenorm(x, g):
    s = x.shape[0]
    tm = 512
    return pl.pallas_call(
        _prenorm_kernel,
        grid=(s // tm,),
        in_specs=[pl.BlockSpec((tm, D_MODEL), lambda i: (i, 0)),
                  pl.BlockSpec((1, D_MODEL), lambda i: (0, 0))],
        out_specs=pl.BlockSpec((tm, D_MODEL), lambda i: (i, 0)),
        out_shape=jax.ShapeDtypeStruct((s, D_MODEL), BF16),
        compiler_params=_params("parallel"),
        name="prenorm",
    )(x, g.reshape(1, D_MODEL))


def _mm_cols_kernel(x_ref, w_ref, s_ref, o_ref, so_ref, wb_ref, *, act, w_is_transposed):
    @pl.when(pl.program_id(1) == 0)
    def _():
        wb_ref[...] = w_ref[...].astype(BF16)

    if w_is_transposed:
        acc = lax.dot_general(x_ref[...], wb_ref[...], NT_DIMS, preferred_element_type=F32)
    else:
        acc = jnp.dot(x_ref[...], wb_ref[...], preferred_element_type=F32)
    if act == "relu2":
        acc = jnp.square(jnp.maximum(acc, 0.0))
    o_ref[...] = acc.astype(o_ref.dtype)
    so_ref[...] = s_ref[...].astype(BF16)


def _side_rows(n_rows, n_steps):
    rows = 16
    while rows * n_steps < n_rows or n_rows % rows:
        rows += 16
    return rows


def _mm_cols(xb, w, layer, n_cols, side, side_layer, act=None, out_dtype=F32, w_is_transposed=False,
             tm=1024, tn=1024):
    m, k = xb.shape
    tm = min(tm, m)
    ni = m // tm
    nj = n_cols // tn
    _, sr, sc = side.shape
    rb = _side_rows(sr, ni * nj)
    last = sr // rb - 1
    side_idx = lambda j, i: jnp.minimum(j * ni + i, last)
    if w_is_transposed:
        w_spec = pl.BlockSpec((None, tn, k), lambda j, i: (layer, j, 0))
        wb_shape = (tn, k)
    else:
        w_spec = pl.BlockSpec((None, k, tn), lambda j, i: (layer, 0, j))
        wb_shape = (k, tn)
    return pl.pallas_call(
        functools.partial(_mm_cols_kernel, act=act, w_is_transposed=w_is_transposed),
        grid=(nj, ni),
        in_specs=[pl.BlockSpec((tm, k), lambda j, i: (i, 0)),
                  w_spec,
                  pl.BlockSpec((None, rb, sc), lambda j, i: (side_layer, side_idx(j, i), 0))],
        out_specs=[pl.BlockSpec((tm, tn), lambda j, i: (i, j)),
                   pl.BlockSpec((rb, sc), lambda j, i: (side_idx(j, i), 0))],
        out_shape=[jax.ShapeDtypeStruct((m, n_cols), out_dtype),
                   jax.ShapeDtypeStruct((sr, sc), BF16)],
        scratch_shapes=[pltpu.VMEM(wb_shape, BF16)],
        compiler_params=_params("arbitrary", "arbitrary"),
        name="mm_cols",
    )(xb, w, side)


def _tail_kernel(x_ref, w_ref, o_ref):
    o_ref[...] = jnp.dot(x_ref[...], w_ref[...].astype(BF16), preferred_element_type=F32)


def _mm_tail(xb, w_tail):
    m, k = xb.shape
    nt = w_tail.shape[1]
    tm = min(512, m)
    return pl.pallas_call(
        _tail_kernel,
        grid=(m // tm,),
        in_specs=[pl.BlockSpec((tm, k), lambda i: (i, 0)),
                  pl.BlockSpec((k, nt), lambda i: (0, 0))],
        out_specs=pl.BlockSpec((tm, nt), lambda i: (i, 0)),
        out_shape=jax.ShapeDtypeStruct((m, nt), F32),
        compiler_params=_params("parallel"),
        name="mm_tail",
    )(xb, w_tail)


def _chunk_rows(col_vals, n):
    s, nh = col_vals.shape
    return jnp.swapaxes(col_vals.reshape(s // n, n, nh), 1, 2)


def _mm_rows_kernel(a_ref, w_ref, x_ref, gp_ref, gn_ref, *refs, nk, with_next):
    xo_ref = refs[0]

    def finish(acc):
        x_new = x_ref[...] + _rms(acc, gp_ref[...])
        xo_ref[...] = x_new
        if with_next:
            refs[1][...] = _rms(x_new, gn_ref[...]).astype(BF16)

    if nk == 1:
        finish(jnp.dot(a_ref[...], w_ref[...], preferred_element_type=F32))
        return

    kk = pl.program_id(1)

    @pl.when(kk == 0)
    def _():
        xo_ref[...] = jnp.zeros_like(xo_ref)

    xo_ref[...] += jnp.dot(a_ref[...], w_ref[...], preferred_element_type=F32)

    @pl.when(kk == nk - 1)
    def _():
        finish(xo_ref[...])


def _mm_rows_tiles(m, k):
    if k <= D_MODEL:
        return min(512, m), k
    return min(1024, m), 1024


def _mm_rows(a, w, x, g_post, g_next):
    m, k = a.shape
    tm, tk = _mm_rows_tiles(m, k)
    nk = k // tk
    with_next = g_next is not None
    if g_next is None:
        g_next = g_post
    row_spec = pl.BlockSpec((tm, D_MODEL), lambda i, kk: (i, 0))
    res_spec = row_spec if nk == 1 else pl.BlockSpec((tm, D_MODEL), lambda i, kk: (i, 0),
                                                      pipeline_mode=pl.Buffered(1))
    vec_spec = pl.BlockSpec((1, D_MODEL), lambda i, kk: (0, 0))
    out_specs = [row_spec]
    out_shape = [jax.ShapeDtypeStruct((m, D_MODEL), F32)]
    if with_next:
        out_specs.append(row_spec)
        out_shape.append(jax.ShapeDtypeStruct((m, D_MODEL), BF16))
    res = pl.pallas_call(
        functools.partial(_mm_rows_kernel, nk=nk, with_next=with_next),
        grid=(m // tm, nk),
        in_specs=[pl.BlockSpec((tm, tk), lambda i, kk: (i, kk)),
                  pl.BlockSpec((tk, D_MODEL), lambda i, kk: (kk, 0)),
                  res_spec, vec_spec, vec_spec],
        out_specs=out_specs,
        out_shape=out_shape,
        compiler_params=_params("parallel", "arbitrary"),
        name="mm_rows",
    )(a, w, x, g_post.reshape(1, D_MODEL), g_next.reshape(1, D_MODEL))
    return (res[0], res[1]) if with_next else (res[0], None)


def _causal_conv(x, halo_ref, cw, first):
    rows = x.shape[0]

    @pl.when(first)
    def _():
        halo_ref[...] = jnp.zeros_like(halo_ref)

    groups = [halo_ref[...]] + [x[g:g + HALO_ROWS] for g in range(0, rows, HALO_ROWS)]
    sub = lax.broadcasted_iota(jnp.int32, groups[0].shape, 0)
    taps = CONV_WIDTH - 1
    rolled = [[pltpu.roll(t, d, axis=0) for t in groups] for d in range(1, taps + 1)]
    out = []
    for g in range(1, len(groups)):
        y = cw[taps:taps + 1] * groups[g]
        for d in range(1, taps + 1):
            shifted = jnp.where(sub < d, rolled[d - 1][g - 1], rolled[d - 1][g])
            y = y + cw[taps - d:taps - d + 1] * shifted
        out.append(y)
    halo_ref[...] = groups[-1]
    return jnp.concatenate(out, axis=0)


GDN_HEAD_GROUP = 16


def _unit_lower_inverses(mats):
    n = mats[0].shape[0]
    row = lax.broadcasted_iota(jnp.int32, (n, n), 0)
    col = lax.broadcasted_iota(jnp.int32, (n, n), 1)
    eye = jnp.where(row == col, 1.0, 0.0).astype(F32)

    def same_block(b):
        shift = int(math.log2(b))
        return (row >> shift) == (col >> shift)

    b = 2
    ds = [eye - jnp.where(same_block(b), a, 0.0) for a in mats]
    while b < n:
        newly = jnp.logical_and(same_block(2 * b), jnp.logical_not(same_block(b)))
        es = [jnp.where(newly, a, 0.0) for a in mats]
        eds = [_mxu(e, d) for e, d in zip(es, ds)]
        ds = [d - _mxu(d, ed) for d, ed in zip(ds, eds)]
        b *= 2
    return ds


def _gdn_kernel(qkv_ref, z_ref, b_ref, a_ref, at_ref, cw_ref, al_ref, dtb_ref, alt_ref, dtbt_ref, ng_ref,
                o_ref, halo_ref, act_ref, s_ref):
    n = GDN_CHUNK
    hdk = GDN_HEADS * GDN_DK
    first = pl.program_id(0) == 0

    @pl.when(first)
    def _():
        s_ref[...] = jnp.zeros_like(s_ref)

    y = _causal_conv(qkv_ref[...], halo_ref, cw_ref[...], first)
    act_ref[...] = _silu(y)

    beta = _sigmoid(b_ref[...])
    g_col = -jnp.exp(al_ref[...]) * _softplus(a_ref[...] + dtb_ref[...])
    g_row = -jnp.exp(alt_ref[...]) * _softplus(at_ref[...] + dtbt_ref[...])
    gc_col, gc_row = _cumsum_both(g_col, g_row, n)
    incl, strict = _tri_masks(n)
    ng = ng_ref[...]

    def l2norm(t):
        return t * lax.rsqrt(jnp.sum(t * t, axis=-1, keepdims=True) + NORM_EPS)

    for g0 in range(0, GDN_HEADS, GDN_HEAD_GROUP):
        heads = range(g0, g0 + GDN_HEAD_GROUP)
        q = [l2norm(act_ref[:, h * GDN_DK:(h + 1) * GDN_DK]) * GDN_DK ** -0.5 for h in heads]
        k = [l2norm(act_ref[:, hdk + h * GDN_DK:hdk + (h + 1) * GDN_DK]) for h in heads]
        v = [act_ref[:, 2 * hdk + h * GDN_DV:2 * hdk + (h + 1) * GDN_DV] for h in heads]
        gcc = [gc_col[:, h:h + 1] for h in heads]
        gcr = [gc_row[h:h + 1, :] for h in heads]
        bc = [beta[:, h:h + 1] for h in heads]
        dec = [jnp.where(incl, jnp.exp(c - r), 0.0) for c, r in zip(gcc, gcr)]
        prod = [_mxu(jnp.concatenate([ki, qi], axis=0), ki, NT_DIMS) for ki, qi in zip(k, q)]
        a_mat = [jnp.where(strict, b * p[:n] * d, 0.0) for b, p, d in zip(bc, prod, dec)]
        t_inv = _unit_lower_inverses(a_mat)
        eg = [jnp.exp(c) for c in gcc]
        rhs = [jnp.concatenate([vi * b, ki * (b * e)], axis=1) for vi, ki, b, e in zip(v, k, bc, eg)]
        uw = [r + _mxu(jnp.where(strict, t, 0.0), r) for t, r in zip(t_inv, rhs)]
        state = [s_ref[h] for h in heads]
        qs_ws = [_mxu(jnp.concatenate([qi * e, x[:, GDN_DV:]], axis=0), st)
                 for qi, e, x, st in zip(q, eg, uw, state)]
        v_new = [x[:, :GDN_DV] - y[n:] for x, y in zip(uw, qs_ws)]
        o = [y[:n] + _mxu(p[n:] * d, vn) for y, p, d, vn in zip(qs_ws, prod, dec, v_new)]
        g_last = [r[:, n - 1:n] for r in gcr]
        upd = [_mxu(ki * jnp.exp(gl - c), vn, TN_DIMS) for ki, gl, c, vn in zip(k, g_last, gcc, v_new)]
        for i, h in enumerate(heads):
            s_ref[h] = state[i] * jnp.exp(g_last[i]) + upd[i]
            zh = z_ref[:, h * GDN_DV:(h + 1) * GDN_DV]
            o_ref[:, h * GDN_DV:(h + 1) * GDN_DV] = (_rms(o[i], ng) * _silu(zh)).astype(BF16)


def _gdn_mixer(hn, w_in, w_out, j, conv_w, a_log, dt_bias, norm_g):
    s = hn.shape[0]
    n = GDN_CHUNK
    nh = GDN_HEADS
    c_qkv = 2 * nh * GDN_DK + nh * GDN_DV
    c_z = nh * GDN_DV
    proj, w_out_b = _mm_cols(hn, jnp.swapaxes(w_in, 1, 2), j, c_qkv + c_z, w_out, j, w_is_transposed=True)
    tail = _mm_tail(hn, w_in[j, :, c_qkv + c_z:])
    b_pre, a_pre = tail[:, :nh], tail[:, nh:]
    a_pre_t = _chunk_rows(a_pre, n)
    full = lambda shape: pl.BlockSpec(shape, lambda i: (0, 0))
    return w_out_b, pl.pallas_call(
        _gdn_kernel,
        grid=(s // n,),
        in_specs=[pl.BlockSpec((n, c_qkv), lambda i: (i, 0)),
                  pl.BlockSpec((n, c_z), lambda i: (i, c_qkv // c_z)),
                  pl.BlockSpec((n, nh), lambda i: (i, 0)),
                  pl.BlockSpec((n, nh), lambda i: (i, 0)),
                  pl.BlockSpec((None, nh, n), lambda i: (i, 0, 0)),
                  full((CONV_WIDTH, c_qkv)),
                  full((1, nh)), full((1, nh)), full((nh, 1)), full((nh, 1)),
                  full((1, GDN_DV))],
        out_specs=pl.BlockSpec((n, c_z), lambda i: (i, 0)),
        out_shape=jax.ShapeDtypeStruct((s, c_z), BF16),
        scratch_shapes=[pltpu.VMEM((HALO_ROWS, c_qkv), F32),
                        pltpu.VMEM((n, c_qkv), F32),
                        pltpu.VMEM((nh, GDN_DK, GDN_DV), F32)],
        compiler_params=_params("arbitrary"),
        name="gdn_chunks",
    )(proj, proj, b_pre, a_pre, a_pre_t, conv_w,
      a_log.reshape(1, nh), dt_bias.reshape(1, nh), a_log.reshape(nh, 1), dt_bias.reshape(nh, 1),
      norm_g.reshape(1, GDN_DV))


def _softcap(t):
    return GATE_SOFTCAP * jnp.tanh(t / GATE_SOFTCAP)


def _mlstm_kernel(p_ref, i_ref, f_ref, it_ref, ft_ref, bi_ref, bf_ref, bit_ref, bft_ref, ng_ref,
                  o_ref, c_ref, n_ref, m_ref):
    n = MLSTM_CHUNK
    nh, dqk, dv = MLSTM_HEADS, MLSTM_DQK, MLSTM_DV

    @pl.when(pl.program_id(0) == 0)
    def _():
        c_ref[...] = jnp.zeros_like(c_ref)
        n_ref[...] = jnp.zeros_like(n_ref)
        m_ref[...] = jnp.zeros_like(m_ref)

    i_col = _softcap(i_ref[...] + bi_ref[...])
    i_row = _softcap(it_ref[...] + bit_ref[...])
    f_col = -_softplus(-_softcap(f_ref[...] + bf_ref[...]))
    f_row = -_softplus(-_softcap(ft_ref[...] + bft_ref[...]))
    b_col, b_row = _cumsum_both(f_col, f_row, n)
    incl, _ = _tri_masks(n)

    heads = range(nh)
    q = [p_ref[:, h * dqk:(h + 1) * dqk] * dqk ** -0.5 for h in heads]
    k = [p_ref[:, nh * dqk + h * dqk:nh * dqk + (h + 1) * dqk] for h in heads]
    v = [p_ref[:, 2 * nh * dqk + h * dv:2 * nh * dqk + (h + 1) * dv] for h in heads]
    bcc = [b_col[:, h:h + 1] for h in heads]
    bcr = [b_row[h:h + 1, :] for h in heads]
    icc = [i_col[:, h:h + 1] for h in heads]
    icr = [i_row[h:h + 1, :] for h in heads]
    m_prev = [m_ref[h:h + 1, 0:1] for h in heads]
    qk = [_mxu(qi, ki, NT_DIMS) for qi, ki in zip(q, k)]
    c_mem = [c_ref[h] for h in heads]
    n_mem = [n_ref[h:h + 1, :] for h in heads]
    qc = [_mxu(qi, c) for qi, c in zip(q, c_mem)]
    d_log = [jnp.where(incl, c - r + i, -jnp.inf) for c, r, i in zip(bcc, bcr, icr)]
    inter = [c + m for c, m in zip(bcc, m_prev)]
    m_t = [jnp.maximum(x, jnp.max(d, axis=-1, keepdims=True)) for x, d in zip(inter, d_log)]
    sc = [s * jnp.exp(d - m) for s, d, m in zip(qk, d_log, m_t)]
    w_inter = [jnp.exp(x - m) for x, m in zip(inter, m_t)]
    num = [w * x + _mxu(s, vi) for w, x, s, vi in zip(w_inter, qc, sc, v)]
    den = [w * jnp.sum(qi * nm, axis=-1, keepdims=True) + jnp.sum(s, axis=-1, keepdims=True)
           for w, qi, nm, s in zip(w_inter, q, n_mem, sc)]
    hid = [x / jnp.maximum(jnp.abs(d), jnp.exp(-m)) for x, d, m in zip(num, den, m_t)]
    b_last = [r[:, n - 1:n] for r in bcr]
    end_max = [jnp.max(bl - r + i, axis=-1, keepdims=True) for bl, r, i in zip(b_last, bcr, icr)]
    m_new = [jnp.maximum(bl + m, e) for bl, m, e in zip(b_last, m_prev, end_max)]
    decay = [jnp.exp(bl + m - mn) for bl, m, mn in zip(b_last, m_prev, m_new)]
    k_w = [ki * jnp.exp(bl - c + i - mn) for ki, bl, c, i, mn in zip(k, b_last, bcc, icc, m_new)]
    upd = [_mxu(kw, vi, TN_DIMS) for kw, vi in zip(k_w, v)]
    for h in heads:
        c_ref[h] = decay[h] * c_mem[h] + upd[h]
        n_ref[h:h + 1, :] = decay[h] * n_mem[h] + jnp.sum(k_w[h], axis=0, keepdims=True)
        m_ref[h:h + 1, :] = jnp.broadcast_to(m_new[h], (1, m_ref.shape[1]))
        o_pre = p_ref[:, 2 * nh * dqk + nh * dv + h * dv:2 * nh * dqk + nh * dv + (h + 1) * dv]
        normed = _rms(hid[h], ng_ref[:, h * dv:(h + 1) * dv])
        o_ref[:, h * dv:(h + 1) * dv] = (normed * _sigmoid(o_pre)).astype(BF16)


def _mlstm_mixer(hn, w_in, w_out, j, b_i, b_f, norm_g):
    s = hn.shape[0]
    n = min(MLSTM_CHUNK, s)
    nh, dqk, dv = MLSTM_HEADS, MLSTM_DQK, MLSTM_DV
    c_main = 2 * nh * dqk + 2 * nh * dv
    proj, w_out_b = _mm_cols(hn, jnp.swapaxes(w_in, 1, 2), j, c_main, w_out, j, w_is_transposed=True)
    tail = _mm_tail(hn, w_in[j, :, c_main:])
    i_pre, f_pre = tail[:, :nh], tail[:, nh:]
    full = lambda shape: pl.BlockSpec(shape, lambda i: (0, 0))
    col = pl.BlockSpec((n, nh), lambda i: (i, 0))
    row = pl.BlockSpec((None, nh, n), lambda i: (i, 0, 0))
    return w_out_b, pl.pallas_call(
        _mlstm_kernel,
        grid=(s // n,),
        in_specs=[pl.BlockSpec((n, c_main), lambda i: (i, 0)), col, col, row, row,
                  full((1, nh)), full((1, nh)), full((nh, 1)), full((nh, 1)), full((1, nh * dv))],
        out_specs=pl.BlockSpec((n, nh * dv), lambda i: (i, 0)),
        out_shape=jax.ShapeDtypeStruct((s, nh * dv), BF16),
        scratch_shapes=[pltpu.VMEM((nh, dqk, dv), F32),
                        pltpu.VMEM((nh, dqk), F32),
                        pltpu.VMEM((nh, 128), F32)],
        compiler_params=_params("arbitrary"),
        name="mlstm_chunks",
    )(proj, i_pre, f_pre, _chunk_rows(i_pre, n), _chunk_rows(f_pre, n),
      b_i.reshape(1, nh), b_f.reshape(1, nh), b_i.reshape(nh, 1), b_f.reshape(nh, 1),
      norm_g.reshape(1, nh * dv))


def _ret_kernel(p_ref, pos_ref, fr_ref, ng_ref, o_ref, r_ref):
    n = RET_CHUNK
    nh, dk, dv = RET_HEADS, RET_DK, RET_DV
    half = dk // 2

    @pl.when(pl.program_id(0) == 0)
    def _():
        r_ref[...] = jnp.zeros_like(r_ref)

    ang = pos_ref[...].astype(F32) * fr_ref[...]
    cos, sin = jnp.cos(ang), jnp.sin(ang)
    row = lax.broadcasted_iota(jnp.int32, (n, n), 0)
    col = lax.broadcasted_iota(jnp.int32, (n, n), 1)
    lag = (row - col).astype(F32)
    t_col = lax.broadcasted_iota(jnp.int32, (n, 1), 0).astype(F32)

    def rope(t):
        t1, t2 = t[:, :half], t[:, half:]
        return jnp.concatenate([t1 * cos - t2 * sin, t1 * sin + t2 * cos], axis=1)

    heads = range(nh)
    log_gamma = [float(np.log1p(-np.exp2(np.float32(-5.0 - h)))) for h in heads]
    q = [rope(p_ref[:, h * dk:(h + 1) * dk]) for h in heads]
    k = [rope(p_ref[:, nh * dk + h * dk:nh * dk + (h + 1) * dk]) * dk ** -0.5 for h in heads]
    v = [p_ref[:, 2 * nh * dk + h * dv:2 * nh * dk + (h + 1) * dv] for h in heads]
    qk = [_mxu(qi, ki, NT_DIMS) for qi, ki in zip(q, k)]
    mem = [r_ref[h] for h in heads]
    qr = [_mxu(qi, m) for qi, m in zip(q, mem)]
    d_mat = [jnp.where(row >= col, jnp.exp(lag * lg), 0.0) for lg in log_gamma]
    intra = [_mxu(s * d, vi) for s, d, vi in zip(qk, d_mat, v)]
    upd = [_mxu(ki * jnp.exp((n - 1.0 - t_col) * lg), vi, TN_DIMS) for ki, lg, vi in zip(k, log_gamma, v)]
    for h in heads:
        o = intra[h] + qr[h] * jnp.exp((t_col + 1.0) * log_gamma[h])
        r_ref[h] = mem[h] * math.exp(n * log_gamma[h]) + upd[h]
        gate = p_ref[:, 2 * nh * dk + nh * dv + h * dv:2 * nh * dk + nh * dv + (h + 1) * dv]
        o = _rms(o, ng_ref[:, h * dv:(h + 1) * dv])
        o_ref[:, h * dv:(h + 1) * dv] = (o * _silu(gate)).astype(BF16)


def _ret_mixer(hn, positions, w_in, w_out, j, norm_g):
    s = hn.shape[0]
    n = min(RET_CHUNK, s)
    nh, dk, dv = RET_HEADS, RET_DK, RET_DV
    c_all = 2 * nh * dk + 2 * nh * dv
    proj, w_out_b = _mm_cols(hn, w_in, j, c_all, w_out, j)
    freqs = ROPE_BASE ** (-jnp.arange(0, dk, 2, dtype=F32) / dk)
    return w_out_b, pl.pallas_call(
        _ret_kernel,
        grid=(s // n,),
        in_specs=[pl.BlockSpec((n, c_all), lambda i: (i, 0)),
                  pl.BlockSpec((n, 1), lambda i: (i, 0)),
                  pl.BlockSpec((1, dk // 2), lambda i: (0, 0)),
                  pl.BlockSpec((1, nh * dv), lambda i: (0, 0))],
        out_specs=pl.BlockSpec((n, nh * dv), lambda i: (i, 0)),
        out_shape=jax.ShapeDtypeStruct((s, nh * dv), BF16),
        scratch_shapes=[pltpu.VMEM((nh, dk, dv), F32)],
        compiler_params=_params("arbitrary"),
        name="ret_chunks",
    )(proj, positions.reshape(s, 1), freqs.reshape(1, dk // 2), norm_g.reshape(1, nh * dv))


def _gelu_tanh(x):
    return 0.5 * x * (1.0 + jnp.tanh(math.sqrt(2.0 / math.pi) * (x + 0.044715 * (x * x * x))))


def _lru_kernel(p_ref, cw_ref, cb_ref, wr_ref, br_ref, wi_ref, bi_ref, lam_ref, o_ref,
                stage_ref, out_ref, halo_ref, h_ref):
    rows = p_ref.shape[0]
    seg = rows // 8
    nb = LRU_BLOCKS
    first = pl.program_id(0) == 0

    @pl.when(first)
    def _():
        h_ref[...] = jnp.zeros_like(h_ref)
        halo_ref[...] = jnp.zeros_like(halo_ref)

    for c in range(2 * nb):
        stage_ref[c] = p_ref[:, c * LRU_BLOCK:(c + 1) * LRU_BLOCK]

    sub = lax.broadcasted_iota(jnp.int32, (8, LRU_BLOCK), 0)
    neg_c_softplus = -LRU_C * _softplus(-lam_ref[...])
    taps = CONV_WIDTH - 1

    for b in range(nb):
        lanes = slice(b * LRU_BLOCK, (b + 1) * LRU_BLOCK)
        xv = [stage_ref[b, pl.ds(v, 8, stride=seg), :] for v in range(seg)]
        wrap = [jnp.where(sub == 0,
                          pltpu.roll(halo_ref[b, taps - d], 1, axis=0),
                          pltpu.roll(xv[seg - d], 1, axis=0)) for d in range(1, taps + 1)]
        for d in range(1, taps + 1):
            halo_ref[b, taps - d] = xv[seg - d]

        def src(v):
            return xv[v] if v >= 0 else wrap[-v - 1]

        cw = cw_ref[:, lanes]
        cb = cb_ref[:, lanes]
        xblk = jnp.concatenate(
            [cw[0:1] * src(v - 3) + cw[1:2] * src(v - 2) + cw[2:3] * src(v - 1) + cw[3:4] * src(v) + cb
             for v in range(seg)], axis=0)
        r = _sigmoid(_mxu(xblk, wr_ref[b]) + br_ref[:, lanes])
        gi = _sigmoid(_mxu(xblk, wi_ref[b]) + bi_ref[:, lanes])
        log_a = r * neg_c_softplus[:, lanes]
        a = jnp.exp(log_a)
        u = jnp.sqrt(1.0 - jnp.exp(2.0 * log_a)) * (gi * xblk)

        hs, ps = [u[0:8]], [a[0:8]]
        for v in range(1, seg):
            av = a[v * 8:(v + 1) * 8]
            hs.append(av * hs[-1] + u[v * 8:(v + 1) * 8])
            ps.append(av * ps[-1])
        e, pc = hs[-1], ps[-1]
        d = 1
        while d < 8:
            keep = sub >= d
            e = e + pc * jnp.where(keep, pltpu.roll(e, d, axis=0), 0.0)
            pc = pc * jnp.where(keep, pltpu.roll(pc, d, axis=0), 1.0)
            d *= 2
        h0 = h_ref[b, 0:1, :]
        end_state = e + pc * h0
        carry = jnp.where(sub == 0, h0, pltpu.roll(end_state, 1, axis=0))
        h_ref[b, 0:1, :] = end_state[7:8]
        for v in range(seg):
            gate = stage_ref[nb + b, pl.ds(v, 8, stride=seg), :]
            out_ref[b, pl.ds(v, 8, stride=seg), :] = (hs[v] + ps[v] * carry) * _gelu_tanh(gate)

    for b in range(nb):
        o_ref[:, b * LRU_BLOCK:(b + 1) * LRU_BLOCK] = out_ref[b].astype(BF16)


def _lru_mixer(hn, w_in, w_out, j, conv_w, conv_b, w_r, b_r, w_i, b_i, lam):
    s = hn.shape[0]
    rows = min(LRU_ROWS, s)
    w = LRU_WIDTH
    proj, w_out_b = _mm_cols(hn, w_in, j, 2 * w, w_out, j)
    vec = pl.BlockSpec((1, w), lambda i: (0, 0))
    blk = pl.BlockSpec((LRU_BLOCKS, LRU_BLOCK, LRU_BLOCK), lambda i: (0, 0, 0))
    return w_out_b, pl.pallas_call(
        _lru_kernel,
        grid=(s // rows,),
        in_specs=[pl.BlockSpec((rows, 2 * w), lambda i: (i, 0)),
                  pl.BlockSpec((CONV_WIDTH, w), lambda i: (0, 0)), vec, blk, vec, blk, vec, vec],
        out_specs=pl.BlockSpec((rows, w), lambda i: (i, 0)),
        out_shape=jax.ShapeDtypeStruct((s, w), BF16),
        scratch_shapes=[pltpu.VMEM((2 * LRU_BLOCKS, rows, LRU_BLOCK), F32),
                        pltpu.VMEM((LRU_BLOCKS, rows, LRU_BLOCK), F32),
                        pltpu.VMEM((LRU_BLOCKS, CONV_WIDTH - 1, 8, LRU_BLOCK), F32),
                        pltpu.VMEM((LRU_BLOCKS, 8, LRU_BLOCK), F32)],
        compiler_params=_params("arbitrary"),
        name="lru_scan",
    )(proj, conv_w, conv_b.reshape(1, w), w_r, b_r.reshape(1, w), w_i, b_i.reshape(1, w), lam.reshape(1, w))


@jax.jit
def kernel(x, positions, norm_g, mlp_w_up, mlp_w_down, gdn_w_in, gdn_conv_w, gdn_a_log, gdn_dt_bias, gdn_norm_g, gdn_w_out, mlstm_w_in, mlstm_b_i, mlstm_b_f, mlstm_norm_g, mlstm_w_out, ret_w_in, ret_norm_g, ret_w_out, lru_w_in, lru_conv_w, lru_conv_b, lru_w_r, lru_b_r, lru_w_i, lru_b_i, lru_lambda, lru_w_out):
    batch, seq, _ = x.shape
    assert batch == 1
    depth = norm_g.shape[0]
    xs = x.reshape(seq, D_MODEL)
    hn = _prenorm(xs, norm_g[0, 0])
    for layer in range(depth):
        kind, j = layer % 4, layer // 4
        if kind == 0:
            w_out, mixed = _gdn_mixer(hn, gdn_w_in, gdn_w_out, j, gdn_conv_w[j], gdn_a_log[j], gdn_dt_bias[j],
                                      gdn_norm_g[j])
        elif kind == 1:
            w_out, mixed = _mlstm_mixer(hn, mlstm_w_in, mlstm_w_out, j, mlstm_b_i[j], mlstm_b_f[j],
                                        mlstm_norm_g[j])
        elif kind == 2:
            w_out, mixed = _ret_mixer(hn, positions, ret_w_in, ret_w_out, j, ret_norm_g[j])
        else:
            w_out, mixed = _lru_mixer(hn, lru_w_in, lru_w_out, j, lru_conv_w[j], lru_conv_b[j], lru_w_r[j],
                                      lru_b_r[j], lru_w_i[j], lru_b_i[j], lru_lambda[j])
        xs, hn = _mm_rows(mixed, w_out, xs, norm_g[layer, 1], norm_g[layer, 2])
        hmid, w_down = _mm_cols(hn, mlp_w_up, layer, D_FF, mlp_w_down, layer, act="relu2", out_dtype=BF16)
        g_next = norm_g[layer + 1, 0] if layer + 1 < depth else None
        xs, hn = _mm_rows(hmid, w_down, xs, norm_g[layer, 3], g_next)
    return xs.reshape(batch, seq, D_MODEL)
```

```python
import functools
import math

import numpy as np
import jax
import jax.numpy as jnp
from jax import lax
from jax.experimental import pallas as pl
from jax.experimental.pallas import tpu as pltpu

F32 = jnp.float32
BF16 = jnp.bfloat16

D_MODEL = 2048
D_FF = 4 * D_MODEL
NORM_EPS = 1e-6
CONV_WIDTH = 4
HALO_ROWS = 8

GDN_HEADS, GDN_DK, GDN_DV = 16, 128, 128
GDN_CHUNK = 64
MLSTM_HEADS, MLSTM_DQK, MLSTM_DV = 8, 128, 256
MLSTM_CHUNK = 128
GATE_SOFTCAP = 15.0
RET_HEADS, RET_DK, RET_DV = 8, 256, 512
RET_CHUNK = 128
ROPE_BASE = 10000.0
LRU_WIDTH, LRU_BLOCKS, LRU_BLOCK = 2048, 16, 128
LRU_ROWS = 128
LRU_C = 8.0

VMEM_LIMIT_BYTES = 56 * 1024 * 1024

NT_DIMS = (((1,), (1,)), ((), ()))
TN_DIMS = (((0,), (0,)), ((), ()))


def _params(*semantics):
    return pltpu.CompilerParams(dimension_semantics=semantics,
                                vmem_limit_bytes=VMEM_LIMIT_BYTES)


def _rms(x, g):
    y = x * lax.rsqrt(jnp.mean(x * x, axis=-1, keepdims=True) + NORM_EPS)
    return y * g


def _sigmoid(x):
    return 1.0 / (1.0 + jnp.exp(-x))


def _silu(x):
    return x * _sigmoid(x)


def _softplus(x):
    return jnp.maximum(x, 0.0) + jnp.log1p(jnp.exp(-jnp.abs(x)))


def _mxu(a, b, dims=None):
    a = a.astype(BF16)
    b = b.astype(BF16)
    if dims is None:
        return jnp.dot(a, b, preferred_element_type=F32)
    return lax.dot_general(a, b, dims, preferred_element_type=F32)


def _split_bf16(a):
    hi = a.astype(BF16)
    lo = (a - hi.astype(F32)).astype(BF16)
    return hi, lo


def _mxu3(a, b):
    m = a.shape[0]
    a_hi, a_lo = _split_bf16(a)
    b_hi, b_lo = _split_bf16(b)
    top = jnp.dot(jnp.concatenate([a_hi, a_lo], axis=0), b_hi, preferred_element_type=F32)
    return top[:m] + top[m:] + jnp.dot(a_hi, b_lo, preferred_element_type=F32)


def _tri_masks(n):
    row = lax.broadcasted_iota(jnp.int32, (n, n), 0)
    col = lax.broadcasted_iota(jnp.int32, (n, n), 1)
    return row >= col, row > col


def _cumsum_both(col_vals, row_vals, n):
    row = lax.broadcasted_iota(jnp.int32, (n, n), 0)
    col = lax.broadcasted_iota(jnp.int32, (n, n), 1)
    lower = jnp.where(row >= col, 1.0, 0.0).astype(F32)
    upper = jnp.where(row <= col, 1.0, 0.0).astype(F32)
    return _mxu3(lower, col_vals), _mxu3(row_vals, upper)


def _prenorm_kernel(x_ref, g_ref, o_ref):
    o_ref[...] = _rms(x_ref[...], g_ref[...]).astype(o_ref.dtype)


def _prenorm(x, g):
    s = x.shape[0]
    tm = 512
    return pl.pallas_call(
        _prenorm_kernel,
        grid=(s // tm,),
        in_specs=[pl.BlockSpec((tm, D_MODEL), lambda i: (i, 0)),
                  pl.BlockSpec((1, D_MODEL), lambda i: (0, 0))],
        out_specs=pl.BlockSpec((tm, D_MODEL), lambda i: (i, 0)),
        out_shape=jax.ShapeDtypeStruct((s, D_MODEL), BF16),
        compiler_params=_params("parallel"),
        name="prenorm",
    )(x, g.reshape(1, D_MODEL))


def _mm_cols_kernel(x_ref, w_ref, s_ref, o_ref, so_ref, wb_ref, *, act, w_is_transposed):
    @pl.when(pl.program_id(1) == 0)
    def _():
        wb_ref[...] = w_ref[...].astype(BF16)

    if w_is_transposed:
        acc = lax.dot_general(x_ref[...], wb_ref[...], NT_DIMS, preferred_element_type=F32)
    else:
        acc = jnp.dot(x_ref[...], wb_ref[...], preferred_element_type=F32)
    if act == "relu2":
        acc = jnp.square(jnp.maximum(acc, 0.0))
    o_ref[...] = acc.astype(o_ref.dtype)
    so_ref[...] = s_ref[...].astype(BF16)


def _side_rows(n_rows, n_steps):
    rows = 16
    while rows * n_steps < n_rows or n_rows % rows:
        rows += 16
    return rows


def _mm_cols(xb, w, layer, n_cols, side, side_layer, act=None, out_dtype=F32, w_is_transposed=False,
             tm=1024, tn=1024):
    m, k = xb.shape
    tm = min(tm, m)
    ni = m // tm
    nj = n_cols // tn
    _, sr, sc = side.shape
    rb = _side_rows(sr, ni * nj)
    last = sr // rb - 1
    side_idx = lambda j, i: jnp.minimum(j * ni + i, last)
    if w_is_transposed:
        w_spec = pl.BlockSpec((None, tn, k), lambda j, i: (layer, j, 0))
        wb_shape = (tn, k)
    else:
        w_spec = pl.BlockSpec((None, k, tn), lambda j, i: (layer, 0, j))
        wb_shape = (k, tn)
    return pl.pallas_call(
        functools.partial(_mm_cols_kernel, act=act, w_is_transposed=w_is_transposed),
        grid=(nj, ni),
        in_specs=[pl.BlockSpec((tm, k), lambda j, i: (i, 0)),
                  w_spec,
                  pl.BlockSpec((None, rb, sc), lambda j, i: (side_layer, side_idx(j, i), 0))],
        out_specs=[pl.BlockSpec((tm, tn), lambda j, i: (i, j)),
                   pl.BlockSpec((rb, sc), lambda j, i: (side_idx(j, i), 0))],
        out_shape=[jax.ShapeDtypeStruct((m, n_cols), out_dtype),
                   jax.ShapeDtypeStruct((sr, sc), BF16)],
        scratch_shapes=[pltpu.VMEM(wb_shape, BF16)],
        compiler_params=_params("arbitrary", "arbitrary"),
        name="mm_cols",
    )(xb, w, side)


def _tail_kernel(x_ref, w_ref, o_ref):
    o_ref[...] = jnp.dot(x_ref[...], w_ref[...].astype(BF16), preferred_element_type=F32)


def _mm_tail(xb, w_tail):
    m, k = xb.shape
    nt = w_tail.shape[1]
    tm = min(512, m)
    return pl.pallas_call(
        _tail_kernel,
        grid=(m // tm,),
        in_specs=[pl.BlockSpec((tm, k), lambda i: (i, 0)),
                  pl.BlockSpec((k, nt), lambda i: (0, 0))],
        out_specs=pl.BlockSpec((tm, nt), lambda i: (i, 0)),
        out_shape=jax.ShapeDtypeStruct((m, nt), F32),
        compiler_params=_params("parallel"),
        name="mm_tail",
    )(xb, w_tail)


def _chunk_rows(col_vals, n):
    s, nh = col_vals.shape
    return jnp.swapaxes(col_vals.reshape(s // n, n, nh), 1, 2)


def _mm_rows_kernel(a_ref, w_ref, x_ref, gp_ref, gn_ref, *refs, nb, nk, with_next):
    xo_ref = refs[0]
    hn_ref = refs[1] if with_next else None
    accs = refs[-2:]
    i = pl.program_id(0)
    k = pl.program_id(1)
    rows = x_ref.shape[0]

    def matmul(acc_ref):
        acc_ref[...] += jnp.dot(a_ref[...], w_ref[...], preferred_element_type=F32)

    def epilogue(acc_ref):
        rows_k = pl.ds(pl.multiple_of(k * rows, rows), rows)
        acc = acc_ref[rows_k, :]
        acc_ref[rows_k, :] = jnp.zeros_like(acc)
        x_new = x_ref[...] + _rms(acc, gp_ref[...])
        xo_ref[...] = x_new
        if with_next:
            hn_ref[...] = _rms(x_new, gn_ref[...]).astype(BF16)

    @pl.when(jnp.logical_and(i == 0, k == 0))
    def _():
        for acc_ref in accs:
            acc_ref[...] = jnp.zeros_like(acc_ref)

    @pl.when(i == 0)
    def _():
        matmul(accs[0])

    for parity in (0, 1):
        @pl.when(jnp.logical_and(jnp.logical_and(i > 0, i < nb), i % 2 == parity))
        def _():
            epilogue(accs[1 - parity])
            matmul(accs[parity])

    @pl.when(i == nb)
    def _():
        epilogue(accs[(nb - 1) % 2])


def _mm_rows_tiles(m, k):
    return min(1024, m), (512 if k <= D_MODEL else 1024)


def _mm_rows(a, w, x, g_post, g_next):
    m, k = a.shape
    tm, tk = _mm_rows_tiles(m, k)
    nb, nk = m // tm, k // tk
    rows = tm // nk
    with_next = g_next is not None
    if g_next is None:
        g_next = g_post
    slice_spec = pl.BlockSpec((rows, D_MODEL), lambda i, kk: (jnp.maximum((i - 1) * nk + kk, 0), 0))
    vec_spec = pl.BlockSpec((1, D_MODEL), lambda i, kk: (0, 0))
    out_specs = [slice_spec]
    out_shape = [jax.ShapeDtypeStruct((m, D_MODEL), F32)]
    if with_next:
        out_specs.append(slice_spec)
        out_shape.append(jax.ShapeDtypeStruct((m, D_MODEL), BF16))
    res = pl.pallas_call(
        functools.partial(_mm_rows_kernel, nb=nb, nk=nk, with_next=with_next),
        grid=(nb + 1, nk),
        in_specs=[pl.BlockSpec((tm, tk), lambda i, kk: (jnp.minimum(i, nb - 1), kk)),
                  pl.BlockSpec((tk, D_MODEL), lambda i, kk: (kk, 0)),
                  slice_spec, vec_spec, vec_spec],
        out_specs=out_specs,
        out_shape=out_shape,
        scratch_shapes=[pltpu.VMEM((tm, D_MODEL), F32), pltpu.VMEM((tm, D_MODEL), F32)],
        compiler_params=_params("arbitrary", "arbitrary"),
        name="mm_rows",
    )(a, w, x, g_post.reshape(1, D_MODEL), g_next.reshape(1, D_MODEL))
    return (res[0], res[1]) if with_next else (res[0], None)


def _causal_conv(x, halo_ref, cw, first):
    rows = x.shape[0]

    @pl.when(first)
    def _():
        halo_ref[...] = jnp.zeros_like(halo_ref)

    groups = [halo_ref[...]] + [x[g:g + HALO_ROWS] for g in range(0, rows, HALO_ROWS)]
    sub = lax.broadcasted_iota(jnp.int32, groups[0].shape, 0)
    taps = CONV_WIDTH - 1
    rolled = [[pltpu.roll(t, d, axis=0) for t in groups] for d in range(1, taps + 1)]
    out = []
    for g in range(1, len(groups)):
        y = cw[taps:taps + 1] * groups[g]
        for d in range(1, taps + 1):
            shifted = jnp.where(sub < d, rolled[d - 1][g - 1], rolled[d - 1][g])
            y = y + cw[taps - d:taps - d + 1] * shifted
        out.append(y)
    halo_ref[...] = groups[-1]
    return jnp.concatenate(out, axis=0)


GDN_HEAD_GROUP = 16


def _unit_lower_inverses(mats):
    n = mats[0].shape[0]
    row = lax.broadcasted_iota(jnp.int32, (n, n), 0)
    col = lax.broadcasted_iota(jnp.int32, (n, n), 1)
    eye = jnp.where(row == col, 1.0, 0.0).astype(F32)

    def same_block(b):
        shift = int(math.log2(b))
        return (row >> shift) == (col >> shift)

    b = 2
    ds = [eye - jnp.where(same_block(b), a, 0.0) for a in mats]
    while b < n:
        newly = jnp.logical_and(same_block(2 * b), jnp.logical_not(same_block(b)))
        es = [jnp.where(newly, a, 0.0) for a in mats]
        eds = [_mxu(e, d) for e, d in zip(es, ds)]
        ds = [d - _mxu(d, ed) for d, ed in zip(ds, eds)]
        b *= 2
    return ds


def _gdn_kernel(qkv_ref, z_ref, b_ref, a_ref, at_ref, cw_ref, al_ref, dtb_ref, alt_ref, dtbt_ref, ng_ref,
                o_ref, halo_ref, act_ref, s_ref):
    n = GDN_CHUNK
    hdk = GDN_HEADS * GDN_DK
    first = pl.program_id(0) == 0

    @pl.when(first)
    def _():
        s_ref[...] = jnp.zeros_like(s_ref)

    y = _causal_conv(qkv_ref[...], halo_ref, cw_ref[...], first)
    act_ref[...] = _silu(y)

    beta = _sigmoid(b_ref[...])
    g_col = -jnp.exp(al_ref[...]) * _softplus(a_ref[...] + dtb_ref[...])
    g_row = -jnp.exp(alt_ref[...]) * _softplus(at_ref[...] + dtbt_ref[...])
    gc_col, gc_row = _cumsum_both(g_col, g_row, n)
    incl, strict = _tri_masks(n)
    ng = ng_ref[...]

    def l2norm(t):
        return t * lax.rsqrt(jnp.sum(t * t, axis=-1, keepdims=True) + NORM_EPS)

    for g0 in range(0, GDN_HEADS, GDN_HEAD_GROUP):
        heads = range(g0, g0 + GDN_HEAD_GROUP)
        q = [l2norm(act_ref[:, h * GDN_DK:(h + 1) * GDN_DK]) * GDN_DK ** -0.5 for h in heads]
        k = [l2norm(act_ref[:, hdk + h * GDN_DK:hdk + (h + 1) * GDN_DK]) for h in heads]
        v = [act_ref[:, 2 * hdk + h * GDN_DV:2 * hdk + (h + 1) * GDN_DV] for h in heads]
        gcc = [gc_col[:, h:h + 1] for h in heads]
        gcr = [gc_row[h:h + 1, :] for h in heads]
        bc = [beta[:, h:h + 1] for h in heads]
        dec = [jnp.where(incl, jnp.exp(c - r), 0.0) for c, r in zip(gcc, gcr)]
        prod = [_mxu(jnp.concatenate([ki, qi], axis=0), ki, NT_DIMS) for ki, qi in zip(k, q)]
        a_mat = [jnp.where(strict, b * p[:n] * d, 0.0) for b, p, d in zip(bc, prod, dec)]
        t_inv = _unit_lower_inverses(a_mat)
        eg = [jnp.exp(c) for c in gcc]
        rhs = [jnp.concatenate([vi * b, ki * (b * e)], axis=1) for vi, ki, b, e in zip(v, k, bc, eg)]
        uw = [r + _mxu(jnp.where(strict, t, 0.0), r) for t, r in zip(t_inv, rhs)]
        state = [s_ref[h] for h in heads]
        qs_ws = [_mxu(jnp.concatenate([qi * e, x[:, GDN_DV:]], axis=0), st)
                 for qi, e, x, st in zip(q, eg, uw, state)]
        v_new = [x[:, :GDN_DV] - y[n:] for x, y in zip(uw, qs_ws)]
        o = [y[:n] + _mxu(p[n:] * d, vn) for y, p, d, vn in zip(qs_ws, prod, dec, v_new)]
        g_last = [r[:, n - 1:n] for r in gcr]
        upd = [_mxu(ki * jnp.exp(gl - c), vn, TN_DIMS) for ki, gl, c, vn in zip(k, g_last, gcc, v_new)]
        for i, h in enumerate(heads):
            s_ref[h] = state[i] * jnp.exp(g_last[i]) + upd[i]
            zh = z_ref[:, h * GDN_DV:(h + 1) * GDN_DV]
            o_ref[:, h * GDN_DV:(h + 1) * GDN_DV] = (_rms(o[i], ng) * _silu(zh)).astype(BF16)


def _gdn_mixer(hn, w_in, w_out, j, conv_w, a_log, dt_bias, norm_g):
    s = hn.shape[0]
    n = GDN_CHUNK
    nh = GDN_HEADS
    c_qkv = 2 * nh * GDN_DK + nh * GDN_DV
    c_z = nh * GDN_DV
    proj, w_out_b = _mm_cols(hn, jnp.swapaxes(w_in, 1, 2), j, c_qkv + c_z, w_out, j, w_is_transposed=True)
    tail = _mm_tail(hn, w_in[j, :, c_qkv + c_z:])
    b_pre, a_pre = tail[:, :nh], tail[:, nh:]
    a_pre_t = _chunk_rows(a_pre, n)
    full = lambda shape: pl.BlockSpec(shape, lambda i: (0, 0))
    return w_out_b, pl.pallas_call(
        _gdn_kernel,
        grid=(s // n,),
        in_specs=[pl.BlockSpec((n, c_qkv), lambda i: (i, 0)),
                  pl.BlockSpec((n, c_z), lambda i: (i, c_qkv // c_z)),
                  pl.BlockSpec((n, nh), lambda i: (i, 0)),
                  pl.BlockSpec((n, nh), lambda i: (i, 0)),
                  pl.BlockSpec((None, nh, n), lambda i: (i, 0, 0)),
                  full((CONV_WIDTH, c_qkv)),
                  full((1, nh)), full((1, nh)), full((nh, 1)), full((nh, 1)),
                  full((1, GDN_DV))],
        out_specs=pl.BlockSpec((n, c_z), lambda i: (i, 0)),
        out_shape=jax.ShapeDtypeStruct((s, c_z), BF16),
        scratch_shapes=[pltpu.VMEM((HALO_ROWS, c_qkv), F32),
                        pltpu.VMEM((n, c_qkv), F32),
                        pltpu.VMEM((nh, GDN_DK, GDN_DV), F32)],
        compiler_params=_params("arbitrary"),
        name="gdn_chunks",
    )(proj, proj, b_pre, a_pre, a_pre_t, conv_w,
      a_log.reshape(1, nh), dt_bias.reshape(1, nh), a_log.reshape(nh, 1), dt_bias.reshape(nh, 1),
      norm_g.reshape(1, GDN_DV))


def _softcap(t):
    return GATE_SOFTCAP * jnp.tanh(t / GATE_SOFTCAP)


def _mlstm_kernel(p_ref, i_ref, f_ref, it_ref, ft_ref, bi_ref, bf_ref, bit_ref, bft_ref, ng_ref,
                  o_ref, c_ref, n_ref, m_ref):
    n = MLSTM_CHUNK
    nh, dqk, dv = MLSTM_HEADS, MLSTM_DQK, MLSTM_DV

    @pl.when(pl.program_id(0) == 0)
    def _():
        c_ref[...] = jnp.zeros_like(c_ref)
        n_ref[...] = jnp.zeros_like(n_ref)
        m_ref[...] = jnp.zeros_like(m_ref)

    i_col = _softcap(i_ref[...] + bi_ref[...])
    i_row = _softcap(it_ref[...] + bit_ref[...])
    f_col = -_softplus(-_softcap(f_ref[...] + bf_ref[...]))
    f_row = -_softplus(-_softcap(ft_ref[...] + bft_ref[...]))
    b_col, b_row = _cumsum_both(f_col, f_row, n)
    incl, _ = _tri_masks(n)

    heads = range(nh)
    q = [p_ref[:, h * dqk:(h + 1) * dqk] * dqk ** -0.5 for h in heads]
    k = [p_ref[:, nh * dqk + h * dqk:nh * dqk + (h + 1) * dqk] for h in heads]
    v = [p_ref[:, 2 * nh * dqk + h * dv:2 * nh * dqk + (h + 1) * dv] for h in heads]
    bcc = [b_col[:, h:h + 1] for h in heads]
    bcr = [b_row[h:h + 1, :] for h in heads]
    icc = [i_col[:, h:h + 1] for h in heads]
    icr = [i_row[h:h + 1, :] for h in heads]
    m_prev = [m_ref[h:h + 1, 0:1] for h in heads]
    qk = [_mxu(qi, ki, NT_DIMS) for qi, ki in zip(q, k)]
    c_mem = [c_ref[h] for h in heads]
    n_mem = [n_ref[h:h + 1, :] for h in heads]
    qc = [_mxu(qi, c) for qi, c in zip(q, c_mem)]
    d_log = [jnp.where(incl, c - r + i, -jnp.inf) for c, r, i in zip(bcc, bcr, icr)]
    inter = [c + m for c, m in zip(bcc, m_prev)]
    m_t = [jnp.maximum(x, jnp.max(d, axis=-1, keepdims=True)) for x, d in zip(inter, d_log)]
    sc = [s * jnp.exp(d - m) for s, d, m in zip(qk, d_log, m_t)]
    w_inter = [jnp.exp(x - m) for x, m in zip(inter, m_t)]
    num = [w * x + _mxu(s, vi) for w, x, s, vi in zip(w_inter, qc, sc, v)]
    den = [w * jnp.sum(qi * nm, axis=-1, keepdims=True) + jnp.sum(s, axis=-1, keepdims=True)
           for w, qi, nm, s in zip(w_inter, q, n_mem, sc)]
    hid = [x / jnp.maximum(jnp.abs(d), jnp.exp(-m)) for x, d, m in zip(num, den, m_t)]
    b_last = [r[:, n - 1:n] for r in bcr]
    end_max = [jnp.max(bl - r + i, axis=-1, keepdims=True) for bl, r, i in zip(b_last, bcr, icr)]
    m_new = [jnp.maximum(bl + m, e) for bl, m, e in zip(b_last, m_prev, end_max)]
    decay = [jnp.exp(bl + m - mn) for bl, m, mn in zip(b_last, m_prev, m_new)]
    k_w = [ki * jnp.exp(bl - c + i - mn) for ki, bl, c, i, mn in zip(k, b_last, bcc, icc, m_new)]
    upd = [_mxu(kw, vi, TN_DIMS) for kw, vi in zip(k_w, v)]
    for h in heads:
        c_ref[h] = decay[h] * c_mem[h] + upd[h]
        n_ref[h:h + 1, :] = decay[h] * n_mem[h] + jnp.sum(k_w[h], axis=0, keepdims=True)
        m_ref[h:h + 1, :] = jnp.broadcast_to(m_new[h], (1, m_ref.shape[1]))
        o_pre = p_ref[:, 2 * nh * dqk + nh * dv + h * dv:2 * nh * dqk + nh * dv + (h + 1) * dv]
        normed = _rms(hid[h], ng_ref[:, h * dv:(h + 1) * dv])
        o_ref[:, h * dv:(h + 1) * dv] = (normed * _sigmoid(o_pre)).astype(BF16)


def _mlstm_mixer(hn, w_in, w_out, j, b_i, b_f, norm_g):
    s = hn.shape[0]
    n = min(MLSTM_CHUNK, s)
    nh, dqk, dv = MLSTM_HEADS, MLSTM_DQK, MLSTM_DV
    c_main = 2 * nh * dqk + 2 * nh * dv
    proj, w_out_b = _mm_cols(hn, jnp.swapaxes(w_in, 1, 2), j, c_main, w_out, j, w_is_transposed=True)
    tail = _mm_tail(hn, w_in[j, :, c_main:])
    i_pre, f_pre = tail[:, :nh], tail[:, nh:]
    full = lambda shape: pl.BlockSpec(shape, lambda i: (0, 0))
    col = pl.BlockSpec((n, nh), lambda i: (i, 0))
    row = pl.BlockSpec((None, nh, n), lambda i: (i, 0, 0))
    return w_out_b, pl.pallas_call(
        _mlstm_kernel,
        grid=(s // n,),
        in_specs=[pl.BlockSpec((n, c_main), lambda i: (i, 0)), col, col, row, row,
                  full((1, nh)), full((1, nh)), full((nh, 1)), full((nh, 1)), full((1, nh * dv))],
        out_specs=pl.BlockSpec((n, nh * dv), lambda i: (i, 0)),
        out_shape=jax.ShapeDtypeStruct((s, nh * dv), BF16),
        scratch_shapes=[pltpu.VMEM((nh, dqk, dv), F32),
                        pltpu.VMEM((nh, dqk), F32),
                        pltpu.VMEM((nh, 128), F32)],
        compiler_params=_params("arbitrary"),
        name="mlstm_chunks",
    )(proj, i_pre, f_pre, _chunk_rows(i_pre, n), _chunk_rows(f_pre, n),
      b_i.reshape(1, nh), b_f.reshape(1, nh), b_i.reshape(nh, 1), b_f.reshape(nh, 1),
      norm_g.reshape(1, nh * dv))


def _ret_kernel(p_ref, pos_ref, fr_ref, ng_ref, o_ref, r_ref):
    n = RET_CHUNK
    nh, dk, dv = RET_HEADS, RET_DK, RET_DV
    half = dk // 2

    @pl.when(pl.program_id(0) == 0)
    def _():
        r_ref[...] = jnp.zeros_like(r_ref)

    ang = pos_ref[...].astype(F32) * fr_ref[...]
    cos, sin = jnp.cos(ang), jnp.sin(ang)
    row = lax.broadcasted_iota(jnp.int32, (n, n), 0)
    col = lax.broadcasted_iota(jnp.int32, (n, n), 1)
    lag = (row - col).astype(F32)
    t_col = lax.broadcasted_iota(jnp.int32, (n, 1), 0).astype(F32)

    def rope(t):
        t1, t2 = t[:, :half], t[:, half:]
        return jnp.concatenate([t1 * cos - t2 * sin, t1 * sin + t2 * cos], axis=1)

    heads = range(nh)
    log_gamma = [float(np.log1p(-np.exp2(np.float32(-5.0 - h)))) for h in heads]
    q = [rope(p_ref[:, h * dk:(h + 1) * dk]) for h in heads]
    k = [rope(p_ref[:, nh * dk + h * dk:nh * dk + (h + 1) * dk]) * dk ** -0.5 for h in heads]
    v = [p_ref[:, 2 * nh * dk + h * dv:2 * nh * dk + (h + 1) * dv] for h in heads]
    qk = [_mxu(qi, ki, NT_DIMS) for qi, ki in zip(q, k)]
    mem = [r_ref[h] for h in heads]
    qr = [_mxu(qi, m) for qi, m in zip(q, mem)]
    d_mat = [jnp.where(row >= col, jnp.exp(lag * lg), 0.0) for lg in log_gamma]
    intra = [_mxu(s * d, vi) for s, d, vi in zip(qk, d_mat, v)]
    upd = [_mxu(ki * jnp.exp((n - 1.0 - t_col) * lg), vi, TN_DIMS) for ki, lg, vi in zip(k, log_gamma, v)]
    for h in heads:
        o = intra[h] + qr[h] * jnp.exp((t_col + 1.0) * log_gamma[h])
        r_ref[h] = mem[h] * math.exp(n * log_gamma[h]) + upd[h]
        gate = p_ref[:, 2 * nh * dk + nh * dv + h * dv:2 * nh * dk + nh * dv + (h + 1) * dv]
        o = _rms(o, ng_ref[:, h * dv:(h + 1) * dv])
        o_ref[:, h * dv:(h + 1) * dv] = (o * _silu(gate)).astype(BF16)


def _ret_mixer(hn, positions, w_in, w_out, j, norm_g):
    s = hn.shape[0]
    n = min(RET_CHUNK, s)
    nh, dk, dv = RET_HEADS, RET_DK, RET_DV
    c_all = 2 * nh * dk + 2 * nh * dv
    proj, w_out_b = _mm_cols(hn, w_in, j, c_all, w_out, j)
    freqs = ROPE_BASE ** (-jnp.arange(0, dk, 2, dtype=F32) / dk)
    return w_out_b, pl.pallas_call(
        _ret_kernel,
        grid=(s // n,),
        in_specs=[pl.BlockSpec((n, c_all), lambda i: (i, 0)),
                  pl.BlockSpec((n, 1), lambda i: (i, 0)),
                  pl.BlockSpec((1, dk // 2), lambda i: (0, 0)),
                  pl.BlockSpec((1, nh * dv), lambda i: (0, 0))],
        out_specs=pl.BlockSpec((n, nh * dv), lambda i: (i, 0)),
        out_shape=jax.ShapeDtypeStruct((s, nh * dv), BF16),
        scratch_shapes=[pltpu.VMEM((nh, dk, dv), F32)],
        compiler_params=_params("arbitrary"),
        name="ret_chunks",
    )(proj, positions.reshape(s, 1), freqs.reshape(1, dk // 2), norm_g.reshape(1, nh * dv))


def _gelu_tanh(x):
    return 0.5 * x * (1.0 + jnp.tanh(math.sqrt(2.0 / math.pi) * (x + 0.044715 * (x * x * x))))


def _lru_kernel(p_ref, cw_ref, cb_ref, wr_ref, br_ref, wi_ref, bi_ref, lam_ref, o_ref,
                stage_ref, out_ref, halo_ref, h_ref):
    rows = p_ref.shape[0]
    seg = rows // 8
    nb = LRU_BLOCKS
    first = pl.program_id(0) == 0

    @pl.when(first)
    def _():
        h_ref[...] = jnp.zeros_like(h_ref)
        halo_ref[...] = jnp.zeros_like(halo_ref)

    for c in range(2 * nb):
        stage_ref[c] = p_ref[:, c * LRU_BLOCK:(c + 1) * LRU_BLOCK]

    sub = lax.broadcasted_iota(jnp.int32, (8, LRU_BLOCK), 0)
    neg_c_softplus = -LRU_C * _softplus(-lam_ref[...])
    taps = CONV_WIDTH - 1

    for b in range(nb):
        lanes = slice(b * LRU_BLOCK, (b + 1) * LRU_BLOCK)
        xv = [stage_ref[b, pl.ds(v, 8, stride=seg), :] for v in range(seg)]
        wrap = [jnp.where(sub == 0,
                          pltpu.roll(halo_ref[b, taps - d], 1, axis=0),
                          pltpu.roll(xv[seg - d], 1, axis=0)) for d in range(1, taps + 1)]
        for d in range(1, taps + 1):
            halo_ref[b, taps - d] = xv[seg - d]

        def src(v):
            return xv[v] if v >= 0 else wrap[-v - 1]

        cw = cw_ref[:, lanes]
        cb = cb_ref[:, lanes]
        xblk = jnp.concatenate(
            [cw[0:1] * src(v - 3) + cw[1:2] * src(v - 2) + cw[2:3] * src(v - 1) + cw[3:4] * src(v) + cb
             for v in range(seg)], axis=0)
        r = _sigmoid(_mxu(xblk, wr_ref[b]) + br_ref[:, lanes])
        gi = _sigmoid(_mxu(xblk, wi_ref[b]) + bi_ref[:, lanes])
        log_a = r * neg_c_softplus[:, lanes]
        a = jnp.exp(log_a)
        u = jnp.sqrt(1.0 - jnp.exp(2.0 * log_a)) * (gi * xblk)

        hs, ps = [u[0:8]], [a[0:8]]
        for v in range(1, seg):
            av = a[v * 8:(v + 1) * 8]
            hs.append(av * hs[-1] + u[v * 8:(v + 1) * 8])
            ps.append(av * ps[-1])
        e, pc = hs[-1], ps[-1]
        d = 1
        while d < 8:
            keep = sub >= d
            e = e + pc * jnp.where(keep, pltpu.roll(e, d, axis=0), 0.0)
            pc = pc * jnp.where(keep, pltpu.roll(pc, d, axis=0), 1.0)
            d *= 2
        h0 = h_ref[b, 0:1, :]
        end_state = e + pc * h0
        carry = jnp.where(sub == 0, h0, pltpu.roll(end_state, 1, axis=0))
        h_ref[b, 0:1, :] = end_state[7:8]
        for v in range(seg):
            gate = stage_ref[nb + b, pl.ds(v, 8, stride=seg), :]
            out_ref[b, pl.ds(v, 8, stride=seg), :] = (hs[v] + ps[v] * carry) * _gelu_tanh(gate)

    for b in range(nb):
        o_ref[:, b * LRU_BLOCK:(b + 1) * LRU_BLOCK] = out_ref[b].astype(BF16)


def _lru_mixer(hn, w_in, w_out, j, conv_w, conv_b, w_r, b_r, w_i, b_i, lam):
    s = hn.shape[0]
    rows = min(LRU_ROWS, s)
    w = LRU_WIDTH
    proj, w_out_b = _mm_cols(hn, w_in, j, 2 * w, w_out, j)
    vec = pl.BlockSpec((1, w), lambda i: (0, 0))
    blk = pl.BlockSpec((LRU_BLOCKS, LRU_BLOCK, LRU_BLOCK), lambda i: (0, 0, 0))
    return w_out_b, pl.pallas_call(
        _lru_kernel,
        grid=(s // rows,),
        in_specs=[pl.BlockSpec((rows, 2 * w), lambda i: (i, 0)),
                  pl.BlockSpec((CONV_WIDTH, w), lambda i: (0, 0)), vec, blk, vec, blk, vec, vec],
        out_specs=pl.BlockSpec((rows, w), lambda i: (i, 0)),
        out_shape=jax.ShapeDtypeStruct((s, w), BF16),
        scratch_shapes=[pltpu.VMEM((2 * LRU_BLOCKS, rows, LRU_BLOCK), F32),
                        pltpu.VMEM((LRU_BLOCKS, rows, LRU_BLOCK), F32),
                        pltpu.VMEM((LRU_BLOCKS, CONV_WIDTH - 1, 8, LRU_BLOCK), F32),
                        pltpu.VMEM((LRU_BLOCKS, 8, LRU_BLOCK), F32)],
        compiler_params=_params("arbitrary"),
        name="lru_scan",
    )(proj, conv_w, conv_b.reshape(1, w), w_r, b_r.reshape(1, w), w_i, b_i.reshape(1, w), lam.reshape(1, w))


@jax.jit
def kernel(x, positions, norm_g, mlp_w_up, mlp_w_down, gdn_w_in, gdn_conv_w, gdn_a_log, gdn_dt_bias, gdn_norm_g, gdn_w_out, mlstm_w_in, mlstm_b_i, mlstm_b_f, mlstm_norm_g, mlstm_w_out, ret_w_in, ret_norm_g, ret_w_out, lru_w_in, lru_conv_w, lru_conv_b, lru_w_r, lru_b_r, lru_w_i, lru_b_i, lru_lambda, lru_w_out):
    batch, seq, _ = x.shape
    assert batch == 1
    depth = norm_g.shape[0]
    xs = x.reshape(seq, D_MODEL)
    hn = _prenorm(xs, norm_g[0, 0])
    for layer in range(depth):
        kind, j = layer % 4, layer // 4
        if kind == 0:
            w_out, mixed = _gdn_mixer(hn, gdn_w_in, gdn_w_out, j, gdn_conv_w[j], gdn_a_log[j], gdn_dt_bias[j],
                                      gdn_norm_g[j])
        elif kind == 1:
            w_out, mixed = _mlstm_mixer(hn, mlstm_w_in, mlstm_w_out, j, mlstm_b_i[j], mlstm_b_f[j],
                                        mlstm_norm_g[j])
        elif kind == 2:
            w_out, mixed = _ret_mixer(hn, positions, ret_w_in, ret_w_out, j, ret_norm_g[j])
        else:
            w_out, mixed = _lru_mixer(hn, lru_w_in, lru_w_out, j, lru_conv_w[j], lru_conv_b[j], lru_w_r[j],
                                      lru_b_r[j], lru_w_i[j], lru_b_i[j], lru_lambda[j])
        xs, hn = _mm_rows(mixed, w_out, xs, norm_g[layer, 1], norm_g[layer, 2])
        hmid, w_down = _mm_cols(hn, mlp_w_up, layer, D_FF, mlp_w_down, layer, act="relu2", out_dtype=BF16)
        g_next = norm_g[layer + 1, 0] if layer + 1 < depth else None
        xs, hn = _mm_rows(hmid, w_down, xs, norm_g[layer, 3], g_next)
    return xs.reshape(batch, seq, D_MODEL)
```

```python
import functools
import math

import numpy as np
import jax
import jax.numpy as jnp
from jax import lax
from jax.experimental import pallas as pl
from jax.experimental.pallas import tpu as pltpu

F32 = jnp.float32
BF16 = jnp.bfloat16

D_MODEL = 2048
D_FF = 4 * D_MODEL
NORM_EPS = 1e-6
CONV_WIDTH = 4
HALO_ROWS = 8

GDN_HEADS, GDN_DK, GDN_DV = 16, 128, 128
GDN_CHUNK = 64
MLSTM_HEADS, MLSTM_DQK, MLSTM_DV = 8, 128, 256
MLSTM_CHUNK = 256
GATE_SOFTCAP = 15.0
RET_HEADS, RET_DK, RET_DV = 8, 256, 512
RET_CHUNK = 256
ROPE_BASE = 10000.0
LRU_WIDTH, LRU_BLOCKS, LRU_BLOCK = 2048, 16, 128
LRU_ROWS = 128
LRU_C = 8.0

VMEM_LIMIT_BYTES = 56 * 1024 * 1024

NT_DIMS = (((1,), (1,)), ((), ()))
TN_DIMS = (((0,), (0,)), ((), ()))


def _params(*semantics):
    return pltpu.CompilerParams(dimension_semantics=semantics,
                                vmem_limit_bytes=VMEM_LIMIT_BYTES)


def _rms(x, g):
    y = x * lax.rsqrt(jnp.mean(x * x, axis=-1, keepdims=True) + NORM_EPS)
    return y * g


def _sigmoid(x):
    return 1.0 / (1.0 + jnp.exp(-x))


def _silu(x):
    return x * _sigmoid(x)


def _softplus(x):
    return jnp.maximum(x, 0.0) + jnp.log1p(jnp.exp(-jnp.abs(x)))


def _mxu(a, b, dims=None):
    a = a.astype(BF16)
    b = b.astype(BF16)
    if dims is None:
        return jnp.dot(a, b, preferred_element_type=F32)
    return lax.dot_general(a, b, dims, preferred_element_type=F32)


def _split_bf16(a):
    hi = a.astype(BF16)
    lo = (a - hi.astype(F32)).astype(BF16)
    return hi, lo


def _mxu3(a, b):
    m = a.shape[0]
    a_hi, a_lo = _split_bf16(a)
    b_hi, b_lo = _split_bf16(b)
    top = jnp.dot(jnp.concatenate([a_hi, a_lo], axis=0), b_hi, preferred_element_type=F32)
    return top[:m] + top[m:] + jnp.dot(a_hi, b_lo, preferred_element_type=F32)


def _tri_masks(n):
    row = lax.broadcasted_iota(jnp.int32, (n, n), 0)
    col = lax.broadcasted_iota(jnp.int32, (n, n), 1)
    return row >= col, row > col


def _cumsum_both(col_vals, row_vals, n):
    row = lax.broadcasted_iota(jnp.int32, (n, n), 0)
    col = lax.broadcasted_iota(jnp.int32, (n, n), 1)
    lower = jnp.where(row >= col, 1.0, 0.0).astype(F32)
    upper = jnp.where(row <= col, 1.0, 0.0).astype(F32)
    return _mxu3(lower, col_vals), _mxu3(row_vals, upper)


def _prenorm_kernel(x_ref, g_ref, o_ref):
    o_ref[...] = _rms(x_ref[...], g_ref[...]).astype(o_ref.dtype)


def _prenorm(x, g):
    s = x.shape[0]
    tm = 512
    return pl.pallas_call(
        _prenorm_kernel,
        grid=(s // tm,),
        in_specs=[pl.BlockSpec((tm, D_MODEL), lambda i: (i, 0)),
                  pl.BlockSpec((1, D_MODEL), lambda i: (0, 0))],
        out_specs=pl.BlockSpec((tm, D_MODEL), lambda i: (i, 0)),
        out_shape=jax.ShapeDtypeStruct((s, D_MODEL), BF16),
        compiler_params=_params("parallel"),
        name="prenorm",
    )(x, g.reshape(1, D_MODEL))


def _mm_cols_kernel(x_ref, w_ref, s_ref, o_ref, so_ref, wb_ref, *, act, w_is_transposed):
    @pl.when(pl.program_id(1) == 0)
    def _():
        wb_ref[...] = w_ref[...].astype(BF16)

    if w_is_transposed:
        acc = lax.dot_general(x_ref[...], wb_ref[...], NT_DIMS, preferred_element_type=F32)
    else:
        acc = jnp.dot(x_ref[...], wb_ref[...], preferred_element_type=F32)
    if act == "relu2":
        acc = jnp.square(jnp.maximum(acc, 0.0))
    o_ref[...] = acc.astype(o_ref.dtype)
    so_ref[...] = s_ref[...].astype(BF16)


def _side_rows(n_rows, n_steps):
    rows = 16
    while rows * n_steps < n_rows or n_rows % rows:
        rows += 16
    return rows


def _mm_cols(xb, w, layer, n_cols, side, side_layer, act=None, out_dtype=F32, w_is_transposed=False,
             tm=1024, tn=1024):
    m, k = xb.shape
    tm = min(tm, m)
    ni = m // tm
    nj = n_cols // tn
    _, sr, sc = side.shape
    rb = _side_rows(sr, ni * nj)
    last = sr // rb - 1
    side_idx = lambda j, i: jnp.minimum(j * ni + i, last)
    if w_is_transposed:
        w_spec = pl.BlockSpec((None, tn, k), lambda j, i: (layer, j, 0))
        wb_shape = (tn, k)
    else:
        w_spec = pl.BlockSpec((None, k, tn), lambda j, i: (layer, 0, j))
        wb_shape = (k, tn)
    return pl.pallas_call(
        functools.partial(_mm_cols_kernel, act=act, w_is_transposed=w_is_transposed),
        grid=(nj, ni),
        in_specs=[pl.BlockSpec((tm, k), lambda j, i: (i, 0)),
                  w_spec,
                  pl.BlockSpec((None, rb, sc), lambda j, i: (side_layer, side_idx(j, i), 0))],
        out_specs=[pl.BlockSpec((tm, tn), lambda j, i: (i, j)),
                   pl.BlockSpec((rb, sc), lambda j, i: (side_idx(j, i), 0))],
        out_shape=[jax.ShapeDtypeStruct((m, n_cols), out_dtype),
                   jax.ShapeDtypeStruct((sr, sc), BF16)],
        scratch_shapes=[pltpu.VMEM(wb_shape, BF16)],
        compiler_params=_params("arbitrary", "arbitrary"),
        name="mm_cols",
    )(xb, w, side)


def _tail_kernel(x_ref, w_ref, o_ref):
    o_ref[...] = jnp.dot(x_ref[...], w_ref[...].astype(BF16), preferred_element_type=F32)


def _mm_tail(xb, w_tail):
    m, k = xb.shape
    nt = w_tail.shape[1]
    tm = min(512, m)
    return pl.pallas_call(
        _tail_kernel,
        grid=(m // tm,),
        in_specs=[pl.BlockSpec((tm, k), lambda i: (i, 0)),
                  pl.BlockSpec((k, nt), lambda i: (0, 0))],
        out_specs=pl.BlockSpec((tm, nt), lambda i: (i, 0)),
        out_shape=jax.ShapeDtypeStruct((m, nt), F32),
        compiler_params=_params("parallel"),
        name="mm_tail",
    )(xb, w_tail)


def _chunk_rows(col_vals, n):
    s, nh = col_vals.shape
    return jnp.swapaxes(col_vals.reshape(s // n, n, nh), 1, 2)


def _mm_rows_kernel(a_ref, w_ref, x_ref, gp_ref, gn_ref, *refs, nb, nk, with_next):
    xo_ref = refs[0]
    hn_ref = refs[1] if with_next else None
    accs = refs[-2:]
    i = pl.program_id(0)
    k = pl.program_id(1)
    rows = x_ref.shape[0]

    def matmul(acc_ref):
        acc_ref[...] += jnp.dot(a_ref[...], w_ref[...], preferred_element_type=F32)

    def epilogue(acc_ref):
        rows_k = pl.ds(pl.multiple_of(k * rows, rows), rows)
        acc = acc_ref[rows_k, :]
        acc_ref[rows_k, :] = jnp.zeros_like(acc)
        x_new = x_ref[...] + _rms(acc, gp_ref[...])
        xo_ref[...] = x_new
        if with_next:
            hn_ref[...] = _rms(x_new, gn_ref[...]).astype(BF16)

    @pl.when(jnp.logical_and(i == 0, k == 0))
    def _():
        for acc_ref in accs:
            acc_ref[...] = jnp.zeros_like(acc_ref)

    @pl.when(i == 0)
    def _():
        matmul(accs[0])

    for parity in (0, 1):
        @pl.when(jnp.logical_and(jnp.logical_and(i > 0, i < nb), i % 2 == parity))
        def _():
            epilogue(accs[1 - parity])
            matmul(accs[parity])

    @pl.when(i == nb)
    def _():
        epilogue(accs[(nb - 1) % 2])


def _mm_rows_tiles(m, k):
    if k <= D_MODEL:
        return min(512, m), k
    return min(1024, m), (2048 if k >= 4 * D_MODEL else 1024)


def _mm_rows(a, w, x, g_post, g_next):
    m, k = a.shape
    tm, tk = _mm_rows_tiles(m, k)
    nb, nk = m // tm, k // tk
    rows = tm // nk
    with_next = g_next is not None
    if g_next is None:
        g_next = g_post
    slice_spec = pl.BlockSpec((rows, D_MODEL), lambda i, kk: (jnp.maximum((i - 1) * nk + kk, 0), 0))
    vec_spec = pl.BlockSpec((1, D_MODEL), lambda i, kk: (0, 0))
    out_specs = [slice_spec]
    out_shape = [jax.ShapeDtypeStruct((m, D_MODEL), F32)]
    if with_next:
        out_specs.append(slice_spec)
        out_shape.append(jax.ShapeDtypeStruct((m, D_MODEL), BF16))
    res = pl.pallas_call(
        functools.partial(_mm_rows_kernel, nb=nb, nk=nk, with_next=with_next),
        grid=(nb + 1, nk),
        in_specs=[pl.BlockSpec((tm, tk), lambda i, kk: (jnp.minimum(i, nb - 1), kk)),
                  pl.BlockSpec((tk, D_MODEL), lambda i, kk: (kk, 0)),
                  slice_spec, vec_spec, vec_spec],
        out_specs=out_specs,
        out_shape=out_shape,
        scratch_shapes=[pltpu.VMEM((tm, D_MODEL), F32), pltpu.VMEM((tm, D_MODEL), F32)],
        compiler_params=_params("arbitrary", "arbitrary"),
        name="mm_rows",
    )(a, w, x, g_post.reshape(1, D_MODEL), g_next.reshape(1, D_MODEL))
    return (res[0], res[1]) if with_next else (res[0], None)


def _causal_conv(x, halo_ref, cw, first):
    rows = x.shape[0]

    @pl.when(first)
    def _():
        halo_ref[...] = jnp.zeros_like(halo_ref)

    groups = [halo_ref[...]] + [x[g:g + HALO_ROWS] for g in range(0, rows, HALO_ROWS)]
    sub = lax.broadcasted_iota(jnp.int32, groups[0].shape, 0)
    taps = CONV_WIDTH - 1
    rolled = [[pltpu.roll(t, d, axis=0) for t in groups] for d in range(1, taps + 1)]
    out = []
    for g in range(1, len(groups)):
        y = cw[taps:taps + 1] * groups[g]
        for d in range(1, taps + 1):
            shifted = jnp.where(sub < d, rolled[d - 1][g - 1], rolled[d - 1][g])
            y = y + cw[taps - d:taps - d + 1] * shifted
        out.append(y)
    halo_ref[...] = groups[-1]
    return jnp.concatenate(out, axis=0)


GDN_HEAD_GROUP = 16


def _unit_lower_inverses(mats):
    n = mats[0].shape[0]
    row = lax.broadcasted_iota(jnp.int32, (n, n), 0)
    col = lax.broadcasted_iota(jnp.int32, (n, n), 1)
    eye = jnp.where(row == col, 1.0, 0.0).astype(F32)

    def same_block(b):
        shift = int(math.log2(b))
        return (row >> shift) == (col >> shift)

    b = 2
    ds = [eye - jnp.where(same_block(b), a, 0.0) for a in mats]
    while b < n:
        newly = jnp.logical_and(same_block(2 * b), jnp.logical_not(same_block(b)))
        es = [jnp.where(newly, a, 0.0) for a in mats]
        eds = [_mxu(e, d) for e, d in zip(es, ds)]
        ds = [d - _mxu(d, ed) for d, ed in zip(ds, eds)]
        b *= 2
    return ds


def _gdn_kernel(qkv_ref, z_ref, b_ref, a_ref, at_ref, cw_ref, al_ref, dtb_ref, alt_ref, dtbt_ref, ng_ref,
                o_ref, halo_ref, act_ref, s_ref):
    n = GDN_CHUNK
    hdk = GDN_HEADS * GDN_DK
    first = pl.program_id(0) == 0

    @pl.when(first)
    def _():
        s_ref[...] = jnp.zeros_like(s_ref)

    y = _causal_conv(qkv_ref[...], halo_ref, cw_ref[...], first)
    act_ref[...] = _silu(y)

    beta = _sigmoid(b_ref[...])
    g_col = -jnp.exp(al_ref[...]) * _softplus(a_ref[...] + dtb_ref[...])
    g_row = -jnp.exp(alt_ref[...]) * _softplus(at_ref[...] + dtbt_ref[...])
    gc_col, gc_row = _cumsum_both(g_col, g_row, n)
    incl, strict = _tri_masks(n)
    ng = ng_ref[...]
    g_last = gc_col[n - 1:n, :]
    eg_all = jnp.exp(gc_col)
    beg_all = beta * eg_all
    kdec_all = jnp.exp(g_last - gc_col)
    sdec_all = jnp.exp(g_last)

    def l2norm(t):
        return t * lax.rsqrt(jnp.sum(t * t, axis=-1, keepdims=True) + NORM_EPS)

    for g0 in range(0, GDN_HEADS, GDN_HEAD_GROUP):
        heads = range(g0, g0 + GDN_HEAD_GROUP)
        q = [l2norm(act_ref[:, h * GDN_DK:(h + 1) * GDN_DK]) * GDN_DK ** -0.5 for h in heads]
        k = [l2norm(act_ref[:, hdk + h * GDN_DK:hdk + (h + 1) * GDN_DK]) for h in heads]
        v = [act_ref[:, 2 * hdk + h * GDN_DV:2 * hdk + (h + 1) * GDN_DV] for h in heads]
        gcc = [gc_col[:, h:h + 1] for h in heads]
        gcr = [gc_row[h:h + 1, :] for h in heads]
        bc = [beta[:, h:h + 1] for h in heads]
        eg = [eg_all[:, h:h + 1] for h in heads]
        dec = [jnp.where(incl, jnp.exp(c - r), 0.0) for c, r in zip(gcc, gcr)]
        prod = [_mxu(jnp.concatenate([ki, qi], axis=0), ki, NT_DIMS) for ki, qi in zip(k, q)]
        a_mat = [jnp.where(strict, b * p[:n] * d, 0.0) for b, p, d in zip(bc, prod, dec)]
        t_inv = _unit_lower_inverses(a_mat)
        rhs = [jnp.concatenate([vi * b, ki * beg_all[:, h:h + 1]], axis=1)
               for vi, ki, b, h in zip(v, k, bc, heads)]
        uw = [r + _mxu(jnp.where(strict, t, 0.0), r) for t, r in zip(t_inv, rhs)]
        state = [s_ref[h] for h in heads]
        qs_ws = [_mxu(jnp.concatenate([qi * e, x[:, GDN_DV:]], axis=0), st)
                 for qi, e, x, st in zip(q, eg, uw, state)]
        v_new = [x[:, :GDN_DV] - y[n:] for x, y in zip(uw, qs_ws)]
        o = [y[:n] + _mxu(p[n:] * d, vn) for y, p, d, vn in zip(qs_ws, prod, dec, v_new)]
        upd = [_mxu(ki * kdec_all[:, h:h + 1], vn, TN_DIMS) for ki, h, vn in zip(k, heads, v_new)]
        for i, h in enumerate(heads):
            s_ref[h] = state[i] * sdec_all[:, h:h + 1] + upd[i]
            zh = z_ref[:, h * GDN_DV:(h + 1) * GDN_DV]
            o_ref[:, h * GDN_DV:(h + 1) * GDN_DV] = (_rms(o[i], ng) * _silu(zh)).astype(BF16)


def _gdn_mixer(hn, w_in, w_out, j, conv_w, a_log, dt_bias, norm_g):
    s = hn.shape[0]
    n = GDN_CHUNK
    nh = GDN_HEADS
    c_qkv = 2 * nh * GDN_DK + nh * GDN_DV
    c_z = nh * GDN_DV
    proj, w_out_b = _mm_cols(hn, jnp.swapaxes(w_in, 1, 2), j, c_qkv + c_z, w_out, j, w_is_transposed=True)
    tail = _mm_tail(hn, w_in[j, :, c_qkv + c_z:])
    b_pre, a_pre = tail[:, :nh], tail[:, nh:]
    a_pre_t = _chunk_rows(a_pre, n)
    full = lambda shape: pl.BlockSpec(shape, lambda i: (0, 0))
    return w_out_b, pl.pallas_call(
        _gdn_kernel,
        grid=(s // n,),
        in_specs=[pl.BlockSpec((n, c_qkv), lambda i: (i, 0)),
                  pl.BlockSpec((n, c_z), lambda i: (i, c_qkv // c_z)),
                  pl.BlockSpec((n, nh), lambda i: (i, 0)),
                  pl.BlockSpec((n, nh), lambda i: (i, 0)),
                  pl.BlockSpec((None, nh, n), lambda i: (i, 0, 0)),
                  full((CONV_WIDTH, c_qkv)),
                  full((1, nh)), full((1, nh)), full((nh, 1)), full((nh, 1)),
                  full((1, GDN_DV))],
        out_specs=pl.BlockSpec((n, c_z), lambda i: (i, 0)),
        out_shape=jax.ShapeDtypeStruct((s, c_z), BF16),
        scratch_shapes=[pltpu.VMEM((HALO_ROWS, c_qkv), F32),
                        pltpu.VMEM((n, c_qkv), F32),
                        pltpu.VMEM((nh, GDN_DK, GDN_DV), F32)],
        compiler_params=_params("arbitrary"),
        name="gdn_chunks",
    )(proj, proj, b_pre, a_pre, a_pre_t, conv_w,
      a_log.reshape(1, nh), dt_bias.reshape(1, nh), a_log.reshape(nh, 1), dt_bias.reshape(nh, 1),
      norm_g.reshape(1, GDN_DV))


def _softcap(t):
    return GATE_SOFTCAP * jnp.tanh(t / GATE_SOFTCAP)


def _mlstm_kernel(p_ref, i_ref, f_ref, it_ref, ft_ref, bi_ref, bf_ref, bit_ref, bft_ref, ng_ref,
                  o_ref, c_ref, n_ref, m_ref):
    n = MLSTM_CHUNK
    nh, dqk, dv = MLSTM_HEADS, MLSTM_DQK, MLSTM_DV

    @pl.when(pl.program_id(0) == 0)
    def _():
        c_ref[...] = jnp.zeros_like(c_ref)
        n_ref[...] = jnp.zeros_like(n_ref)
        m_ref[...] = jnp.zeros_like(m_ref)

    i_col = _softcap(i_ref[...] + bi_ref[...])
    i_row = _softcap(it_ref[...] + bit_ref[...])
    f_col = -_softplus(-_softcap(f_ref[...] + bf_ref[...]))
    f_row = -_softplus(-_softcap(ft_ref[...] + bft_ref[...]))
    b_col, b_row = _cumsum_both(f_col, f_row, n)
    incl, _ = _tri_masks(n)

    heads = range(nh)
    q = [p_ref[:, h * dqk:(h + 1) * dqk] * dqk ** -0.5 for h in heads]
    k = [p_ref[:, nh * dqk + h * dqk:nh * dqk + (h + 1) * dqk] for h in heads]
    v = [p_ref[:, 2 * nh * dqk + h * dv:2 * nh * dqk + (h + 1) * dv] for h in heads]
    bcc = [b_col[:, h:h + 1] for h in heads]
    bcr = [b_row[h:h + 1, :] for h in heads]
    icc = [i_col[:, h:h + 1] for h in heads]
    icr = [i_row[h:h + 1, :] for h in heads]
    m_prev = [m_ref[h:h + 1, 0:1] for h in heads]
    qk = [_mxu(qi, ki, NT_DIMS) for qi, ki in zip(q, k)]
    c_mem = [c_ref[h] for h in heads]
    n_mem = [n_ref[h:h + 1, :] for h in heads]
    qc = [_mxu(qi, c) for qi, c in zip(q, c_mem)]
    d_log = [jnp.where(incl, c - r + i, -jnp.inf) for c, r, i in zip(bcc, bcr, icr)]
    inter = [c + m for c, m in zip(bcc, m_prev)]
    m_t = [jnp.maximum(x, jnp.max(d, axis=-1, keepdims=True)) for x, d in zip(inter, d_log)]
    sc = [s * jnp.exp(d - m) for s, d, m in zip(qk, d_log, m_t)]
    w_inter = [jnp.exp(x - m) for x, m in zip(inter, m_t)]
    num = [w * x + _mxu(s, vi) for w, x, s, vi in zip(w_inter, qc, sc, v)]
    den = [w * jnp.sum(qi * nm, axis=-1, keepdims=True) + jnp.sum(s, axis=-1, keepdims=True)
           for w, qi, nm, s in zip(w_inter, q, n_mem, sc)]
    hid = [x / jnp.maximum(jnp.abs(d), jnp.exp(-m)) for x, d, m in zip(num, den, m_t)]
    b_last = [r[:, n - 1:n] for r in bcr]
    end_max = [jnp.max(bl - r + i, axis=-1, keepdims=True) for bl, r, i in zip(b_last, bcr, icr)]
    m_new = [jnp.maximum(bl + m, e) for bl, m, e in zip(b_last, m_prev, end_max)]
    decay = [jnp.exp(bl + m - mn) for bl, m, mn in zip(b_last, m_prev, m_new)]
    k_w = [ki * jnp.exp(bl - c + i - mn) for ki, bl, c, i, mn in zip(k, b_last, bcc, icc, m_new)]
    upd = [_mxu(kw, vi, TN_DIMS) for kw, vi in zip(k_w, v)]
    for h in heads:
        c_ref[h] = decay[h] * c_mem[h] + upd[h]
        n_ref[h:h + 1, :] = decay[h] * n_mem[h] + jnp.sum(k_w[h], axis=0, keepdims=True)
        m_ref[h:h + 1, :] = jnp.broadcast_to(m_new[h], (1, m_ref.shape[1]))
        o_pre = p_ref[:, 2 * nh * dqk + nh * dv + h * dv:2 * nh * dqk + nh * dv + (h + 1) * dv]
        normed = _rms(hid[h], ng_ref[:, h * dv:(h + 1) * dv])
        o_ref[:, h * dv:(h + 1) * dv] = (normed * _sigmoid(o_pre)).astype(BF16)


def _mlstm_mixer(hn, w_in, w_out, j, b_i, b_f, norm_g):
    s = hn.shape[0]
    n = min(MLSTM_CHUNK, s)
    nh, dqk, dv = MLSTM_HEADS, MLSTM_DQK, MLSTM_DV
    c_main = 2 * nh * dqk + 2 * nh * dv
    proj, w_out_b = _mm_cols(hn, jnp.swapaxes(w_in, 1, 2), j, c_main, w_out, j, w_is_transposed=True)
    tail = _mm_tail(hn, w_in[j, :, c_main:])
    i_pre, f_pre = tail[:, :nh], tail[:, nh:]
    full = lambda shape: pl.BlockSpec(shape, lambda i: (0, 0))
    col = pl.BlockSpec((n, nh), lambda i: (i, 0))
    row = pl.BlockSpec((None, nh, n), lambda i: (i, 0, 0))
    return w_out_b, pl.pallas_call(
        _mlstm_kernel,
        grid=(s // n,),
        in_specs=[pl.BlockSpec((n, c_main), lambda i: (i, 0)), col, col, row, row,
                  full((1, nh)), full((1, nh)), full((nh, 1)), full((nh, 1)), full((1, nh * dv))],
        out_specs=pl.BlockSpec((n, nh * dv), lambda i: (i, 0)),
        out_shape=jax.ShapeDtypeStruct((s, nh * dv), BF16),
        scratch_shapes=[pltpu.VMEM((nh, dqk, dv), F32),
                        pltpu.VMEM((nh, dqk), F32),
                        pltpu.VMEM((nh, 128), F32)],
        compiler_params=_params("arbitrary"),
        name="mlstm_chunks",
    )(proj, i_pre, f_pre, _chunk_rows(i_pre, n), _chunk_rows(f_pre, n),
      b_i.reshape(1, nh), b_f.reshape(1, nh), b_i.reshape(nh, 1), b_f.reshape(nh, 1),
      norm_g.reshape(1, nh * dv))


def _ret_kernel(p_ref, pos_ref, fr_ref, ng_ref, o_ref, r_ref):
    n = RET_CHUNK
    nh, dk, dv = RET_HEADS, RET_DK, RET_DV
    half = dk // 2

    @pl.when(pl.program_id(0) == 0)
    def _():
        r_ref[...] = jnp.zeros_like(r_ref)

    ang = pos_ref[...].astype(F32) * fr_ref[...]
    cos, sin = jnp.cos(ang), jnp.sin(ang)
    row = lax.broadcasted_iota(jnp.int32, (n, n), 0)
    col = lax.broadcasted_iota(jnp.int32, (n, n), 1)
    lag = (row - col).astype(F32)
    t_col = lax.broadcasted_iota(jnp.int32, (n, 1), 0).astype(F32)

    def rope(t):
        t1, t2 = t[:, :half], t[:, half:]
        return jnp.concatenate([t1 * cos - t2 * sin, t1 * sin + t2 * cos], axis=1)

    heads = range(nh)
    log_gamma = [float(np.log1p(-np.exp2(np.float32(-5.0 - h)))) for h in heads]
    q = [rope(p_ref[:, h * dk:(h + 1) * dk]) for h in heads]
    k = [rope(p_ref[:, nh * dk + h * dk:nh * dk + (h + 1) * dk]) * dk ** -0.5 for h in heads]
    v = [p_ref[:, 2 * nh * dk + h * dv:2 * nh * dk + (h + 1) * dv] for h in heads]
    qk = [_mxu(qi, ki, NT_DIMS) for qi, ki in zip(q, k)]
    mem = [r_ref[h] for h in heads]
    qr = [_mxu(qi, m) for qi, m in zip(q, mem)]
    d_mat = [jnp.where(row >= col, jnp.exp(lag * lg), 0.0) for lg in log_gamma]
    intra = [_mxu(s * d, vi) for s, d, vi in zip(qk, d_mat, v)]
    upd = [_mxu(ki * jnp.exp((n - 1.0 - t_col) * lg), vi, TN_DIMS) for ki, lg, vi in zip(k, log_gamma, v)]
    for h in heads:
        o = intra[h] + qr[h] * jnp.exp((t_col + 1.0) * log_gamma[h])
        r_ref[h] = mem[h] * math.exp(n * log_gamma[h]) + upd[h]
        gate = p_ref[:, 2 * nh * dk + nh * dv + h * dv:2 * nh * dk + nh * dv + (h + 1) * dv]
        o = _rms(o, ng_ref[:, h * dv:(h + 1) * dv])
        o_ref[:, h * dv:(h + 1) * dv] = (o * _silu(gate)).astype(BF16)


def _ret_mixer(hn, positions, w_in, w_out, j, norm_g):
    s = hn.shape[0]
    n = min(RET_CHUNK, s)
    nh, dk, dv = RET_HEADS, RET_DK, RET_DV
    c_all = 2 * nh * dk + 2 * nh * dv
    proj, w_out_b = _mm_cols(hn, w_in, j, c_all, w_out, j)
    freqs = ROPE_BASE ** (-jnp.arange(0, dk, 2, dtype=F32) / dk)
    return w_out_b, pl.pallas_call(
        _ret_kernel,
        grid=(s // n,),
        in_specs=[pl.BlockSpec((n, c_all), lambda i: (i, 0)),
                  pl.BlockSpec((n, 1), lambda i: (i, 0)),
                  pl.BlockSpec((1, dk // 2), lambda i: (0, 0)),
                  pl.BlockSpec((1, nh * dv), lambda i: (0, 0))],
        out_specs=pl.BlockSpec((n, nh * dv), lambda i: (i, 0)),
        out_shape=jax.ShapeDtypeStruct((s, nh * dv), BF16),
        scratch_shapes=[pltpu.VMEM((nh, dk, dv), F32)],
        compiler_params=_params("arbitrary"),
        name="ret_chunks",
    )(proj, positions.reshape(s, 1), freqs.reshape(1, dk // 2), norm_g.reshape(1, nh * dv))


def _gelu_tanh(x):
    return 0.5 * x * (1.0 + jnp.tanh(math.sqrt(2.0 / math.pi) * (x + 0.044715 * (x * x * x))))


def _lru_kernel(p_ref, cw_ref, cb_ref, wr_ref, br_ref, wi_ref, bi_ref, lam_ref, o_ref,
                stage_ref, out_ref, halo_ref, h_ref):
    rows = p_ref.shape[0]
    seg = rows // 8
    nb = LRU_BLOCKS
    first = pl.program_id(0) == 0

    @pl.when(first)
    def _():
        h_ref[...] = jnp.zeros_like(h_ref)
        halo_ref[...] = jnp.zeros_like(halo_ref)

    for c in range(2 * nb):
        stage_ref[c] = p_ref[:, c * LRU_BLOCK:(c + 1) * LRU_BLOCK]

    sub = lax.broadcasted_iota(jnp.int32, (8, LRU_BLOCK), 0)
    neg_c_softplus = -LRU_C * _softplus(-lam_ref[...])
    taps = CONV_WIDTH - 1

    for b in range(nb):
        lanes = slice(b * LRU_BLOCK, (b + 1) * LRU_BLOCK)
        xv = [stage_ref[b, pl.ds(v, 8, stride=seg), :] for v in range(seg)]
        wrap = [jnp.where(sub == 0,
                          pltpu.roll(halo_ref[b, taps - d], 1, axis=0),
                          pltpu.roll(xv[seg - d], 1, axis=0)) for d in range(1, taps + 1)]
        for d in range(1, taps + 1):
            halo_ref[b, taps - d] = xv[seg - d]

        def src(v):
            return xv[v] if v >= 0 else wrap[-v - 1]

        cw = cw_ref[:, lanes]
        cb = cb_ref[:, lanes]
        xblk = jnp.concatenate(
            [cw[0:1] * src(v - 3) + cw[1:2] * src(v - 2) + cw[2:3] * src(v - 1) + cw[3:4] * src(v) + cb
             for v in range(seg)], axis=0)
        r = _sigmoid(_mxu(xblk, wr_ref[b]) + br_ref[:, lanes])
        gi = _sigmoid(_mxu(xblk, wi_ref[b]) + bi_ref[:, lanes])
        log_a = r * neg_c_softplus[:, lanes]
        a = jnp.exp(log_a)
        u = jnp.sqrt(1.0 - jnp.exp(2.0 * log_a)) * (gi * xblk)

        hs, ps = [u[0:8]], [a[0:8]]
        for v in range(1, seg):
            av = a[v * 8:(v + 1) * 8]
            hs.append(av * hs[-1] + u[v * 8:(v + 1) * 8])
            ps.append(av * ps[-1])
        e, pc = hs[-1], ps[-1]
        d = 1
        while d < 8:
            keep = sub >= d
            e = e + pc * jnp.where(keep, pltpu.roll(e, d, axis=0), 0.0)
            pc = pc * jnp.where(keep, pltpu.roll(pc, d, axis=0), 1.0)
            d *= 2
        h0 = h_ref[b, 0:1, :]
        end_state = e + pc * h0
        carry = jnp.where(sub == 0, h0, pltpu.roll(end_state, 1, axis=0))
        h_ref[b, 0:1, :] = end_state[7:8]
        for v in range(seg):
            gate = stage_ref[nb + b, pl.ds(v, 8, stride=seg), :]
            out_ref[b, pl.ds(v, 8, stride=seg), :] = (hs[v] + ps[v] * carry) * _gelu_tanh(gate)

    for b in range(nb):
        o_ref[:, b * LRU_BLOCK:(b + 1) * LRU_BLOCK] = out_ref[b].astype(BF16)


def _lru_mixer(hn, w_in, w_out, j, conv_w, conv_b, w_r, b_r, w_i, b_i, lam):
    s = hn.shape[0]
    rows = min(LRU_ROWS, s)
    w = LRU_WIDTH
    proj, w_out_b = _mm_cols(hn, w_in, j, 2 * w, w_out, j)
    vec = pl.BlockSpec((1, w), lambda i: (0, 0))
    blk = pl.BlockSpec((LRU_BLOCKS, LRU_BLOCK, LRU_BLOCK), lambda i: (0, 0, 0))
    return w_out_b, pl.pallas_call(
        _lru_kernel,
        grid=(s // rows,),
        in_specs=[pl.BlockSpec((rows, 2 * w), lambda i: (i, 0)),
                  pl.BlockSpec((CONV_WIDTH, w), lambda i: (0, 0)), vec, blk, vec, blk, vec, vec],
        out_specs=pl.BlockSpec((rows, w), lambda i: (i, 0)),
        out_shape=jax.ShapeDtypeStruct((s, w), BF16),
        scratch_shapes=[pltpu.VMEM((2 * LRU_BLOCKS, rows, LRU_BLOCK), F32),
                        pltpu.VMEM((LRU_BLOCKS, rows, LRU_BLOCK), F32),
                        pltpu.VMEM((LRU_BLOCKS, CONV_WIDTH - 1, 8, LRU_BLOCK), F32),
                        pltpu.VMEM((LRU_BLOCKS, 8, LRU_BLOCK), F32)],
        compiler_params=_params("arbitrary"),
        name="lru_scan",
    )(proj, conv_w, conv_b.reshape(1, w), w_r, b_r.reshape(1, w), w_i, b_i.reshape(1, w), lam.reshape(1, w))


@jax.jit
def kernel(x, positions, norm_g, mlp_w_up, mlp_w_down, gdn_w_in, gdn_conv_w, gdn_a_log, gdn_dt_bias, gdn_norm_g, gdn_w_out, mlstm_w_in, mlstm_b_i, mlstm_b_f, mlstm_norm_g, mlstm_w_out, ret_w_in, ret_norm_g, ret_w_out, lru_w_in, lru_conv_w, lru_conv_b, lru_w_r, lru_b_r, lru_w_i, lru_b_i, lru_lambda, lru_w_out):
    batch, seq, _ = x.shape
    assert batch == 1
    depth = norm_g.shape[0]
    xs = x.reshape(seq, D_MODEL)
    hn = _prenorm(xs, norm_g[0, 0])
    for layer in range(depth):
        kind, j = layer % 4, layer // 4
        if kind == 0:
            w_out, mixed = _gdn_mixer(hn, gdn_w_in, gdn_w_out, j, gdn_conv_w[j], gdn_a_log[j], gdn_dt_bias[j],
                                      gdn_norm_g[j])
        elif kind == 1:
            w_out, mixed = _mlstm_mixer(hn, mlstm_w_in, mlstm_w_out, j, mlstm_b_i[j], mlstm_b_f[j],
                                        mlstm_norm_g[j])
        elif kind == 2:
            w_out, mixed = _ret_mixer(hn, positions, ret_w_in, ret_w_out, j, ret_norm_g[j])
        else:
            w_out, mixed = _lru_mixer(hn, lru_w_in, lru_w_out, j, lru_conv_w[j], lru_conv_b[j], lru_w_r[j],
                                      lru_b_r[j], lru_w_i[j], lru_b_i[j], lru_lambda[j])
        xs, hn = _mm_rows(mixed, w_out, xs, norm_g[layer, 1], norm_g[layer, 2])
        hmid, w_down = _mm_cols(hn, mlp_w_up, layer, D_FF, mlp_w_down, layer, act="relu2", out_dtype=BF16)
        g_next = norm_g[layer + 1, 0] if layer + 1 < depth else None
        xs, hn = _mm_rows(hmid, w_down, xs, norm_g[layer, 3], g_next)
    return xs.reshape(batch, seq, D_MODEL)
```

```python
import functools
import math

import numpy as np
import jax
import jax.numpy as jnp
from jax import lax
from jax.experimental import pallas as pl
from jax.experimental.pallas import tpu as pltpu

F32 = jnp.float32
BF16 = jnp.bfloat16

D_MODEL = 2048
D_FF = 4 * D_MODEL
NORM_EPS = 1e-6
CONV_WIDTH = 4
HALO_ROWS = 8

GDN_HEADS, GDN_DK, GDN_DV = 16, 128, 128
GDN_CHUNK = 64
MLSTM_HEADS, MLSTM_DQK, MLSTM_DV = 8, 128, 256
MLSTM_CHUNK = 256
GATE_SOFTCAP = 15.0
RET_HEADS, RET_DK, RET_DV = 8, 256, 512
RET_CHUNK = 256
ROPE_BASE = 10000.0
LRU_WIDTH, LRU_BLOCKS, LRU_BLOCK = 2048, 16, 128
LRU_ROWS = 128
LRU_C = 8.0

VMEM_LIMIT_BYTES = 56 * 1024 * 1024

NT_DIMS = (((1,), (1,)), ((), ()))
TN_DIMS = (((0,), (0,)), ((), ()))


def _params(*semantics):
    return pltpu.CompilerParams(dimension_semantics=semantics,
                                vmem_limit_bytes=VMEM_LIMIT_BYTES)


def _rms(x, g):
    y = x * lax.rsqrt(jnp.mean(x * x, axis=-1, keepdims=True) + NORM_EPS)
    return y * g


def _sigmoid(x):
    return 1.0 / (1.0 + jnp.exp(-x))


def _silu(x):
    return x * _sigmoid(x)


def _softplus(x):
    return jnp.maximum(x, 0.0) + jnp.log1p(jnp.exp(-jnp.abs(x)))


def _mxu(a, b, dims=None):
    a = a.astype(BF16)
    b = b.astype(BF16)
    if dims is None:
        return jnp.dot(a, b, preferred_element_type=F32)
    return lax.dot_general(a, b, dims, preferred_element_type=F32)


def _split_bf16(a):
    hi = a.astype(BF16)
    lo = (a - hi.astype(F32)).astype(BF16)
    return hi, lo


def _mxu3(a, b):
    m = a.shape[0]
    a_hi, a_lo = _split_bf16(a)
    b_hi, b_lo = _split_bf16(b)
    top = jnp.dot(jnp.concatenate([a_hi, a_lo], axis=0), b_hi, preferred_element_type=F32)
    return top[:m] + top[m:] + jnp.dot(a_hi, b_lo, preferred_element_type=F32)


def _tri_masks(n):
    row = lax.broadcasted_iota(jnp.int32, (n, n), 0)
    col = lax.broadcasted_iota(jnp.int32, (n, n), 1)
    return row >= col, row > col


def _cumsum_both(col_vals, row_vals, n):
    row = lax.broadcasted_iota(jnp.int32, (n, n), 0)
    col = lax.broadcasted_iota(jnp.int32, (n, n), 1)
    lower = jnp.where(row >= col, 1.0, 0.0).astype(F32)
    upper = jnp.where(row <= col, 1.0, 0.0).astype(F32)
    return _mxu3(lower, col_vals), _mxu3(row_vals, upper)


def _prenorm_kernel(x_ref, g_ref, o_ref):
    o_ref[...] = _rms(x_ref[...], g_ref[...]).astype(o_ref.dtype)


def _prenorm(x, g):
    s = x.shape[0]
    tm = 512
    return pl.pallas_call(
        _prenorm_kernel,
        grid=(s // tm,),
        in_specs=[pl.BlockSpec((tm, D_MODEL), lambda i: (i, 0)),
                  pl.BlockSpec((1, D_MODEL), lambda i: (0, 0))],
        out_specs=pl.BlockSpec((tm, D_MODEL), lambda i: (i, 0)),
        out_shape=jax.ShapeDtypeStruct((s, D_MODEL), BF16),
        compiler_params=_params("parallel"),
        name="prenorm",
    )(x, g.reshape(1, D_MODEL))


def _mm_cols_kernel(x_ref, w_ref, s_ref, o_ref, so_ref, wb_ref, *, act, w_is_transposed):
    @pl.when(pl.program_id(1) == 0)
    def _():
        wb_ref[...] = w_ref[...].astype(BF16)

    if w_is_transposed:
        acc = lax.dot_general(x_ref[...], wb_ref[...], NT_DIMS, preferred_element_type=F32)
    else:
        acc = jnp.dot(x_ref[...], wb_ref[...], preferred_element_type=F32)
    if act == "relu2":
        acc = jnp.square(jnp.maximum(acc, 0.0))
    o_ref[...] = acc.astype(o_ref.dtype)
    so_ref[...] = s_ref[...].astype(BF16)


def _side_rows(n_rows, n_steps):
    rows = 16
    while rows * n_steps < n_rows or n_rows % rows:
        rows += 16
    return rows


def _mm_cols(xb, w, layer, n_cols, side, side_layer, act=None, out_dtype=F32, w_is_transposed=False,
             tm=1024, tn=1024):
    m, k = xb.shape
    tm = min(tm, m)
    ni = m // tm
    nj = n_cols // tn
    _, sr, sc = side.shape
    rb = _side_rows(sr, ni * nj)
    last = sr // rb - 1
    side_idx = lambda j, i: jnp.minimum(j * ni + i, last)
    if w_is_transposed:
        w_spec = pl.BlockSpec((None, tn, k), lambda j, i: (layer, j, 0))
        wb_shape = (tn, k)
    else:
        w_spec = pl.BlockSpec((None, k, tn), lambda j, i: (layer, 0, j))
        wb_shape = (k, tn)
    return pl.pallas_call(
        functools.partial(_mm_cols_kernel, act=act, w_is_transposed=w_is_transposed),
        grid=(nj, ni),
        in_specs=[pl.BlockSpec((tm, k), lambda j, i: (i, 0)),
                  w_spec,
                  pl.BlockSpec((None, rb, sc), lambda j, i: (side_layer, side_idx(j, i), 0))],
        out_specs=[pl.BlockSpec((tm, tn), lambda j, i: (i, j)),
                   pl.BlockSpec((rb, sc), lambda j, i: (side_idx(j, i), 0))],
        out_shape=[jax.ShapeDtypeStruct((m, n_cols), out_dtype),
                   jax.ShapeDtypeStruct((sr, sc), BF16)],
        scratch_shapes=[pltpu.VMEM(wb_shape, BF16)],
        compiler_params=_params("arbitrary", "arbitrary"),
        name="mm_cols",
    )(xb, w, side)


def _tail_kernel(x_ref, w_ref, o_ref):
    o_ref[...] = jnp.dot(x_ref[...], w_ref[...].astype(BF16), preferred_element_type=F32)


def _mm_tail(xb, w_tail):
    m, k = xb.shape
    nt = w_tail.shape[1]
    tm = min(512, m)
    return pl.pallas_call(
        _tail_kernel,
        grid=(m // tm,),
        in_specs=[pl.BlockSpec((tm, k), lambda i: (i, 0)),
                  pl.BlockSpec((k, nt), lambda i: (0, 0))],
        out_specs=pl.BlockSpec((tm, nt), lambda i: (i, 0)),
        out_shape=jax.ShapeDtypeStruct((m, nt), F32),
        compiler_params=_params("parallel"),
        name="mm_tail",
    )(xb, w_tail)


def _chunk_rows(col_vals, n):
    s, nh = col_vals.shape
    return jnp.swapaxes(col_vals.reshape(s // n, n, nh), 1, 2)


def _mm_rows_kernel(a_ref, w_ref, x_ref, gp_ref, gn_ref, *refs, nb, nk, with_next):
    xo_ref = refs[0]
    hn_ref = refs[1] if with_next else None
    accs = refs[-2:]
    i = pl.program_id(0)
    k = pl.program_id(1)
    rows = x_ref.shape[0]

    def matmul(acc_ref):
        acc_ref[...] += jnp.dot(a_ref[...], w_ref[...], preferred_element_type=F32)

    def epilogue(acc_ref):
        rows_k = pl.ds(pl.multiple_of(k * rows, rows), rows)
        acc = acc_ref[rows_k, :]
        acc_ref[rows_k, :] = jnp.zeros_like(acc)
        x_new = x_ref[...] + _rms(acc, gp_ref[...])
        xo_ref[...] = x_new
        if with_next:
            hn_ref[...] = _rms(x_new, gn_ref[...]).astype(BF16)

    @pl.when(jnp.logical_and(i == 0, k == 0))
    def _():
        for acc_ref in accs:
            acc_ref[...] = jnp.zeros_like(acc_ref)

    @pl.when(i == 0)
    def _():
        matmul(accs[0])

    for parity in (0, 1):
        @pl.when(jnp.logical_and(jnp.logical_and(i > 0, i < nb), i % 2 == parity))
        def _():
            epilogue(accs[1 - parity])
            matmul(accs[parity])

    @pl.when(i == nb)
    def _():
        epilogue(accs[(nb - 1) % 2])


def _mm_rows_tiles(m, k):
    if k <= D_MODEL:
        return min(512, m), k
    return min(1024, m), (2048 if k >= 4 * D_MODEL else 1024)


def _mm_rows(a, w, x, g_post, g_next):
    m, k = a.shape
    tm, tk = _mm_rows_tiles(m, k)
    nb, nk = m // tm, k // tk
    rows = tm // nk
    with_next = g_next is not None
    if g_next is None:
        g_next = g_post
    slice_spec = pl.BlockSpec((rows, D_MODEL), lambda i, kk: (jnp.maximum((i - 1) * nk + kk, 0), 0))
    vec_spec = pl.BlockSpec((1, D_MODEL), lambda i, kk: (0, 0))
    out_specs = [slice_spec]
    out_shape = [jax.ShapeDtypeStruct((m, D_MODEL), F32)]
    if with_next:
        out_specs.append(slice_spec)
        out_shape.append(jax.ShapeDtypeStruct((m, D_MODEL), BF16))
    res = pl.pallas_call(
        functools.partial(_mm_rows_kernel, nb=nb, nk=nk, with_next=with_next),
        grid=(nb + 1, nk),
        in_specs=[pl.BlockSpec((tm, tk), lambda i, kk: (jnp.minimum(i, nb - 1), kk)),
                  pl.BlockSpec((tk, D_MODEL), lambda i, kk: (kk, 0)),
                  slice_spec, vec_spec, vec_spec],
        out_specs=out_specs,
        out_shape=out_shape,
        scratch_shapes=[pltpu.VMEM((tm, D_MODEL), F32), pltpu.VMEM((tm, D_MODEL), F32)],
        compiler_params=_params("arbitrary", "arbitrary"),
        name="mm_rows",
    )(a, w, x, g_post.reshape(1, D_MODEL), g_next.reshape(1, D_MODEL))
    return (res[0], res[1]) if with_next else (res[0], None)


OUT_PROJ_PIECES = 8


class _Interleaved:
    def __init__(self, pieces):
        self._pieces = list(pieces)

    def __call__(self):
        if self._pieces:
            self._pieces.pop(0)()

    def flush(self):
        while self._pieces:
            self._pieces.pop(0)()


def _fused_out_proj_kernel(*refs, chunk_body, n_chunk_in):
    chunk_in = refs[:n_chunk_in]
    w_ref, x_ref, gp_ref, gn_ref, xo_ref, hn_ref, mixed_ref, h_ref = refs[n_chunk_in:n_chunk_in + 8]
    chunk_scratch = refs[n_chunk_in + 8:]

    @pl.when(pl.program_id(0) == 0)
    def _():
        mixed_ref[...] = jnp.zeros_like(mixed_ref)

    cols = D_MODEL // OUT_PROJ_PIECES

    def piece(c):
        def run():
            h_ref[:, c * cols:(c + 1) * cols] = jnp.dot(mixed_ref[...], w_ref[:, c * cols:(c + 1) * cols],
                                                        preferred_element_type=F32)
        return run

    chunk_body(*chunk_in, mixed_ref, *chunk_scratch, tick=_Interleaved(piece(c) for c in range(OUT_PROJ_PIECES)))
    x_new = x_ref[...] + _rms(h_ref[...], gp_ref[...])
    xo_ref[...] = x_new
    hn_ref[...] = _rms(x_new, gn_ref[...]).astype(BF16)


def _chunks_with_out_proj(chunk_body, n_steps, rows, chunk_specs, chunk_args, chunk_scratch,
                          w_out, x, g_post, g_next, name):
    k = w_out.shape[0]
    m = x.shape[0]
    trail = pl.BlockSpec((rows, D_MODEL), lambda i: (jnp.maximum(i - 1, 0), 0))
    vec = pl.BlockSpec((1, D_MODEL), lambda i: (0, 0))
    xo, hn = pl.pallas_call(
        functools.partial(_fused_out_proj_kernel, chunk_body=chunk_body, n_chunk_in=len(chunk_specs)),
        grid=(n_steps + 1,),
        in_specs=list(chunk_specs) + [
            pl.BlockSpec((k, D_MODEL), lambda i: (0, 0), pipeline_mode=pl.Buffered(1)), trail, vec, vec],
        out_specs=[trail, trail],
        out_shape=[jax.ShapeDtypeStruct((m, D_MODEL), F32), jax.ShapeDtypeStruct((m, D_MODEL), BF16)],
        scratch_shapes=[pltpu.VMEM((rows, k), BF16), pltpu.VMEM((rows, D_MODEL), F32)] + list(chunk_scratch),
        compiler_params=_params("arbitrary"),
        name=name,
    )(*chunk_args, w_out, x, g_post.reshape(1, D_MODEL), g_next.reshape(1, D_MODEL))
    return xo, hn


def _causal_conv(x, halo_ref, cw, first):
    rows = x.shape[0]

    @pl.when(first)
    def _():
        halo_ref[...] = jnp.zeros_like(halo_ref)

    groups = [halo_ref[...]] + [x[g:g + HALO_ROWS] for g in range(0, rows, HALO_ROWS)]
    sub = lax.broadcasted_iota(jnp.int32, groups[0].shape, 0)
    taps = CONV_WIDTH - 1
    rolled = [[pltpu.roll(t, d, axis=0) for t in groups] for d in range(1, taps + 1)]
    out = []
    for g in range(1, len(groups)):
        y = cw[taps:taps + 1] * groups[g]
        for d in range(1, taps + 1):
            shifted = jnp.where(sub < d, rolled[d - 1][g - 1], rolled[d - 1][g])
            y = y + cw[taps - d:taps - d + 1] * shifted
        out.append(y)
    halo_ref[...] = groups[-1]
    return jnp.concatenate(out, axis=0)


GDN_HEAD_GROUP = 16


def _unit_lower_inverses(mats):
    n = mats[0].shape[0]
    row = lax.broadcasted_iota(jnp.int32, (n, n), 0)
    col = lax.broadcasted_iota(jnp.int32, (n, n), 1)
    eye = jnp.where(row == col, 1.0, 0.0).astype(F32)

    def same_block(b):
        shift = int(math.log2(b))
        return (row >> shift) == (col >> shift)

    b = 2
    ds = [eye - jnp.where(same_block(b), a, 0.0) for a in mats]
    while b < n:
        newly = jnp.logical_and(same_block(2 * b), jnp.logical_not(same_block(b)))
        es = [jnp.where(newly, a, 0.0) for a in mats]
        eds = [_mxu(e, d) for e, d in zip(es, ds)]
        ds = [d - _mxu(d, ed) for d, ed in zip(ds, eds)]
        b *= 2
    return ds


def _gdn_kernel(qkv_ref, z_ref, b_ref, a_ref, at_ref, cw_ref, al_ref, dtb_ref, alt_ref, dtbt_ref, ng_ref,
                o_ref, halo_ref, act_ref, s_ref):
    n = GDN_CHUNK
    hdk = GDN_HEADS * GDN_DK
    first = pl.program_id(0) == 0

    @pl.when(first)
    def _():
        s_ref[...] = jnp.zeros_like(s_ref)

    y = _causal_conv(qkv_ref[...], halo_ref, cw_ref[...], first)
    act_ref[...] = _silu(y)

    beta = _sigmoid(b_ref[...])
    g_col = -jnp.exp(al_ref[...]) * _softplus(a_ref[...] + dtb_ref[...])
    g_row = -jnp.exp(alt_ref[...]) * _softplus(at_ref[...] + dtbt_ref[...])
    gc_col, gc_row = _cumsum_both(g_col, g_row, n)
    incl, strict = _tri_masks(n)
    ng = ng_ref[...]
    g_last = gc_col[n - 1:n, :]
    eg_all = jnp.exp(gc_col)
    beg_all = beta * eg_all
    kdec_all = jnp.exp(g_last - gc_col)
    sdec_all = jnp.exp(g_last)

    def l2norm(t):
        return t * lax.rsqrt(jnp.sum(t * t, axis=-1, keepdims=True) + NORM_EPS)

    for g0 in range(0, GDN_HEADS, GDN_HEAD_GROUP):
        heads = range(g0, g0 + GDN_HEAD_GROUP)
        q = [l2norm(act_ref[:, h * GDN_DK:(h + 1) * GDN_DK]) * GDN_DK ** -0.5 for h in heads]
        k = [l2norm(act_ref[:, hdk + h * GDN_DK:hdk + (h + 1) * GDN_DK]) for h in heads]
        v = [act_ref[:, 2 * hdk + h * GDN_DV:2 * hdk + (h + 1) * GDN_DV] for h in heads]
        gcc = [gc_col[:, h:h + 1] for h in heads]
        gcr = [gc_row[h:h + 1, :] for h in heads]
        bc = [beta[:, h:h + 1] for h in heads]
        eg = [eg_all[:, h:h + 1] for h in heads]
        dec = [jnp.where(incl, jnp.exp(c - r), 0.0) for c, r in zip(gcc, gcr)]
        prod = [_mxu(jnp.concatenate([ki, qi], axis=0), ki, NT_DIMS) for ki, qi in zip(k, q)]
        a_mat = [jnp.where(strict, b * p[:n] * d, 0.0) for b, p, d in zip(bc, prod, dec)]
        t_inv = _unit_lower_inverses(a_mat)
        rhs = [jnp.concatenate([vi * b, ki * beg_all[:, h:h + 1]], axis=1)
               for vi, ki, b, h in zip(v, k, bc, heads)]
        uw = [r + _mxu(jnp.where(strict, t, 0.0), r) for t, r in zip(t_inv, rhs)]
        state = [s_ref[h] for h in heads]
        qs_ws = [_mxu(jnp.concatenate([qi * e, x[:, GDN_DV:]], axis=0), st)
                 for qi, e, x, st in zip(q, eg, uw, state)]
        v_new = [x[:, :GDN_DV] - y[n:] for x, y in zip(uw, qs_ws)]
        o = [y[:n] + _mxu(p[n:] * d, vn) for y, p, d, vn in zip(qs_ws, prod, dec, v_new)]
        upd = [_mxu(ki * kdec_all[:, h:h + 1], vn, TN_DIMS) for ki, h, vn in zip(k, heads, v_new)]
        for i, h in enumerate(heads):
            s_ref[h] = state[i] * sdec_all[:, h:h + 1] + upd[i]
            zh = z_ref[:, h * GDN_DV:(h + 1) * GDN_DV]
            o_ref[:, h * GDN_DV:(h + 1) * GDN_DV] = (_rms(o[i], ng) * _silu(zh)).astype(BF16)


def _gdn_mixer(hn, w_in, w_out, j, conv_w, a_log, dt_bias, norm_g):
    s = hn.shape[0]
    n = GDN_CHUNK
    nh = GDN_HEADS
    c_qkv = 2 * nh * GDN_DK + nh * GDN_DV
    c_z = nh * GDN_DV
    proj, w_out_b = _mm_cols(hn, jnp.swapaxes(w_in, 1, 2), j, c_qkv + c_z, w_out, j, w_is_transposed=True)
    tail = _mm_tail(hn, w_in[j, :, c_qkv + c_z:])
    b_pre, a_pre = tail[:, :nh], tail[:, nh:]
    a_pre_t = _chunk_rows(a_pre, n)
    full = lambda shape: pl.BlockSpec(shape, lambda i: (0, 0))
    return w_out_b, pl.pallas_call(
        _gdn_kernel,
        grid=(s // n,),
        in_specs=[pl.BlockSpec((n, c_qkv), lambda i: (i, 0)),
                  pl.BlockSpec((n, c_z), lambda i: (i, c_qkv // c_z)),
                  pl.BlockSpec((n, nh), lambda i: (i, 0)),
                  pl.BlockSpec((n, nh), lambda i: (i, 0)),
                  pl.BlockSpec((None, nh, n), lambda i: (i, 0, 0)),
                  full((CONV_WIDTH, c_qkv)),
                  full((1, nh)), full((1, nh)), full((nh, 1)), full((nh, 1)),
                  full((1, GDN_DV))],
        out_specs=pl.BlockSpec((n, c_z), lambda i: (i, 0)),
        out_shape=jax.ShapeDtypeStruct((s, c_z), BF16),
        scratch_shapes=[pltpu.VMEM((HALO_ROWS, c_qkv), F32),
                        pltpu.VMEM((n, c_qkv), F32),
                        pltpu.VMEM((nh, GDN_DK, GDN_DV), F32)],
        compiler_params=_params("arbitrary"),
        name="gdn_chunks",
    )(proj, proj, b_pre, a_pre, a_pre_t, conv_w,
      a_log.reshape(1, nh), dt_bias.reshape(1, nh), a_log.reshape(nh, 1), dt_bias.reshape(nh, 1),
      norm_g.reshape(1, GDN_DV))


def _softcap(t):
    return GATE_SOFTCAP * jnp.tanh(t / GATE_SOFTCAP)


def _mlstm_kernel(p_ref, i_ref, f_ref, it_ref, ft_ref, bi_ref, bf_ref, bit_ref, bft_ref, ng_ref,
                  o_ref, c_ref, n_ref, m_ref, *, tick):
    n = MLSTM_CHUNK
    nh, dqk, dv = MLSTM_HEADS, MLSTM_DQK, MLSTM_DV

    @pl.when(pl.program_id(0) == 0)
    def _():
        c_ref[...] = jnp.zeros_like(c_ref)
        n_ref[...] = jnp.zeros_like(n_ref)
        m_ref[...] = jnp.zeros_like(m_ref)

    i_col = _softcap(i_ref[...] + bi_ref[...])
    i_row = _softcap(it_ref[...] + bit_ref[...])
    f_col = -_softplus(-_softcap(f_ref[...] + bf_ref[...]))
    f_row = -_softplus(-_softcap(ft_ref[...] + bft_ref[...]))
    b_col, b_row = _cumsum_both(f_col, f_row, n)
    incl, _ = _tri_masks(n)

    heads = range(nh)
    q = [p_ref[:, h * dqk:(h + 1) * dqk] * dqk ** -0.5 for h in heads]
    k = [p_ref[:, nh * dqk + h * dqk:nh * dqk + (h + 1) * dqk] for h in heads]
    v = [p_ref[:, 2 * nh * dqk + h * dv:2 * nh * dqk + (h + 1) * dv] for h in heads]
    bcc = [b_col[:, h:h + 1] for h in heads]
    bcr = [b_row[h:h + 1, :] for h in heads]
    icc = [i_col[:, h:h + 1] for h in heads]
    icr = [i_row[h:h + 1, :] for h in heads]
    m_prev = [m_ref[h:h + 1, 0:1] for h in heads]
    qk = [_mxu(qi, ki, NT_DIMS) for qi, ki in zip(q, k)]
    c_mem = [c_ref[h] for h in heads]
    n_mem = [n_ref[h:h + 1, :] for h in heads]
    qc = [_mxu(qi, c) for qi, c in zip(q, c_mem)]
    d_log = [jnp.where(incl, c - r + i, -jnp.inf) for c, r, i in zip(bcc, bcr, icr)]
    tick()
    inter = [c + m for c, m in zip(bcc, m_prev)]
    m_t = [jnp.maximum(x, jnp.max(d, axis=-1, keepdims=True)) for x, d in zip(inter, d_log)]
    sc = [s * jnp.exp(d - m) for s, d, m in zip(qk, d_log, m_t)]
    tick()
    w_inter = [jnp.exp(x - m) for x, m in zip(inter, m_t)]
    num = [w * x + _mxu(s, vi) for w, x, s, vi in zip(w_inter, qc, sc, v)]
    tick()
    den = [w * jnp.sum(qi * nm, axis=-1, keepdims=True) + jnp.sum(s, axis=-1, keepdims=True)
           for w, qi, nm, s in zip(w_inter, q, n_mem, sc)]
    hid = [x / jnp.maximum(jnp.abs(d), jnp.exp(-m)) for x, d, m in zip(num, den, m_t)]
    tick()
    b_last = [r[:, n - 1:n] for r in bcr]
    end_max = [jnp.max(bl - r + i, axis=-1, keepdims=True) for bl, r, i in zip(b_last, bcr, icr)]
    m_new = [jnp.maximum(bl + m, e) for bl, m, e in zip(b_last, m_prev, end_max)]
    decay = [jnp.exp(bl + m - mn) for bl, m, mn in zip(b_last, m_prev, m_new)]
    k_w = [ki * jnp.exp(bl - c + i - mn) for ki, bl, c, i, mn in zip(k, b_last, bcc, icc, m_new)]
    upd = [_mxu(kw, vi, TN_DIMS) for kw, vi in zip(k_w, v)]
    outs = []
    for h in heads:
        c_ref[h] = decay[h] * c_mem[h] + upd[h]
        n_ref[h:h + 1, :] = decay[h] * n_mem[h] + jnp.sum(k_w[h], axis=0, keepdims=True)
        m_ref[h:h + 1, :] = jnp.broadcast_to(m_new[h], (1, m_ref.shape[1]))
        o_pre = p_ref[:, 2 * nh * dqk + nh * dv + h * dv:2 * nh * dqk + nh * dv + (h + 1) * dv]
        normed = _rms(hid[h], ng_ref[:, h * dv:(h + 1) * dv])
        outs.append((normed * _sigmoid(o_pre)).astype(BF16))
        tick()
    tick.flush()
    for h in heads:
        o_ref[:, h * dv:(h + 1) * dv] = outs[h]


def _mlstm_mixer(hn, w_in, w_out, j, b_i, b_f, norm_g, x, g_post, g_next):
    s = hn.shape[0]
    n = min(MLSTM_CHUNK, s)
    n_steps = s // n
    nh, dqk, dv = MLSTM_HEADS, MLSTM_DQK, MLSTM_DV
    c_main = 2 * nh * dqk + 2 * nh * dv
    proj, w_out_b = _mm_cols(hn, jnp.swapaxes(w_in, 1, 2), j, c_main, w_out, j, w_is_transposed=True)
    tail = _mm_tail(hn, w_in[j, :, c_main:])
    i_pre, f_pre = tail[:, :nh], tail[:, nh:]
    step = lambda i: jnp.minimum(i, n_steps - 1)
    full = lambda shape: pl.BlockSpec(shape, lambda i: (0, 0))
    col = pl.BlockSpec((n, nh), lambda i: (step(i), 0))
    row = pl.BlockSpec((None, nh, n), lambda i: (step(i), 0, 0))
    return _chunks_with_out_proj(
        _mlstm_kernel, n_steps, n,
        [pl.BlockSpec((n, c_main), lambda i: (step(i), 0)), col, col, row, row,
         full((1, nh)), full((1, nh)), full((nh, 1)), full((nh, 1)), full((1, nh * dv))],
        (proj, i_pre, f_pre, _chunk_rows(i_pre, n), _chunk_rows(f_pre, n),
         b_i.reshape(1, nh), b_f.reshape(1, nh), b_i.reshape(nh, 1), b_f.reshape(nh, 1),
         norm_g.reshape(1, nh * dv)),
        [pltpu.VMEM((nh, dqk, dv), F32), pltpu.VMEM((nh, dqk), F32), pltpu.VMEM((nh, 128), F32)],
        w_out_b, x, g_post, g_next, "mlstm_chunks")


def _ret_kernel(p_ref, pos_ref, fr_ref, ng_ref, o_ref, r_ref):
    n = RET_CHUNK
    nh, dk, dv = RET_HEADS, RET_DK, RET_DV
    half = dk // 2

    @pl.when(pl.program_id(0) == 0)
    def _():
        r_ref[...] = jnp.zeros_like(r_ref)

    ang = pos_ref[...].astype(F32) * fr_ref[...]
    cos, sin = jnp.cos(ang), jnp.sin(ang)
    row = lax.broadcasted_iota(jnp.int32, (n, n), 0)
    col = lax.broadcasted_iota(jnp.int32, (n, n), 1)
    lag = (row - col).astype(F32)
    t_col = lax.broadcasted_iota(jnp.int32, (n, 1), 0).astype(F32)

    def rope(t):
        t1, t2 = t[:, :half], t[:, half:]
        return jnp.concatenate([t1 * cos - t2 * sin, t1 * sin + t2 * cos], axis=1)

    heads = range(nh)
    log_gamma = [float(np.log1p(-np.exp2(np.float32(-5.0 - h)))) for h in heads]
    q = [rope(p_ref[:, h * dk:(h + 1) * dk]) for h in heads]
    k = [rope(p_ref[:, nh * dk + h * dk:nh * dk + (h + 1) * dk]) * dk ** -0.5 for h in heads]
    v = [p_ref[:, 2 * nh * dk + h * dv:2 * nh * dk + (h + 1) * dv] for h in heads]
    qk = [_mxu(qi, ki, NT_DIMS) for qi, ki in zip(q, k)]
    mem = [r_ref[h] for h in heads]
    qr = [_mxu(qi, m) for qi, m in zip(q, mem)]
    d_mat = [jnp.where(row >= col, jnp.exp(lag * lg), 0.0) for lg in log_gamma]
    intra = [_mxu(s * d, vi) for s, d, vi in zip(qk, d_mat, v)]
    upd = [_mxu(ki * jnp.exp((n - 1.0 - t_col) * lg), vi, TN_DIMS) for ki, lg, vi in zip(k, log_gamma, v)]
    for h in heads:
        o = intra[h] + qr[h] * jnp.exp((t_col + 1.0) * log_gamma[h])
        r_ref[h] = mem[h] * math.exp(n * log_gamma[h]) + upd[h]
        gate = p_ref[:, 2 * nh * dk + nh * dv + h * dv:2 * nh * dk + nh * dv + (h + 1) * dv]
        o = _rms(o, ng_ref[:, h * dv:(h + 1) * dv])
        o_ref[:, h * dv:(h + 1) * dv] = (o * _silu(gate)).astype(BF16)


def _ret_mixer(hn, positions, w_in, w_out, j, norm_g):
    s = hn.shape[0]
    n = min(RET_CHUNK, s)
    nh, dk, dv = RET_HEADS, RET_DK, RET_DV
    c_all = 2 * nh * dk + 2 * nh * dv
    proj, w_out_b = _mm_cols(hn, w_in, j, c_all, w_out, j)
    freqs = ROPE_BASE ** (-jnp.arange(0, dk, 2, dtype=F32) / dk)
    return w_out_b, pl.pallas_call(
        _ret_kernel,
        grid=(s // n,),
        in_specs=[pl.BlockSpec((n, c_all), lambda i: (i, 0)),
                  pl.BlockSpec((n, 1), lambda i: (i, 0)),
                  pl.BlockSpec((1, dk // 2), lambda i: (0, 0)),
                  pl.BlockSpec((1, nh * dv), lambda i: (0, 0))],
        out_specs=pl.BlockSpec((n, nh * dv), lambda i: (i, 0)),
        out_shape=jax.ShapeDtypeStruct((s, nh * dv), BF16),
        scratch_shapes=[pltpu.VMEM((nh, dk, dv), F32)],
        compiler_params=_params("arbitrary"),
        name="ret_chunks",
    )(proj, positions.reshape(s, 1), freqs.reshape(1, dk // 2), norm_g.reshape(1, nh * dv))


def _gelu_tanh(x):
    return 0.5 * x * (1.0 + jnp.tanh(math.sqrt(2.0 / math.pi) * (x + 0.044715 * (x * x * x))))


def _lru_kernel(p_ref, cw_ref, cb_ref, wr_ref, br_ref, wi_ref, bi_ref, lam_ref, o_ref,
                stage_ref, out_ref, halo_ref, h_ref, *, tick):
    rows = p_ref.shape[0]
    seg = rows // 8
    nb = LRU_BLOCKS
    first = pl.program_id(0) == 0

    @pl.when(first)
    def _():
        h_ref[...] = jnp.zeros_like(h_ref)
        halo_ref[...] = jnp.zeros_like(halo_ref)

    for c in range(2 * nb):
        stage_ref[c] = p_ref[:, c * LRU_BLOCK:(c + 1) * LRU_BLOCK]

    sub = lax.broadcasted_iota(jnp.int32, (8, LRU_BLOCK), 0)
    neg_c_softplus = -LRU_C * _softplus(-lam_ref[...])
    taps = CONV_WIDTH - 1

    for b in range(nb):
        lanes = slice(b * LRU_BLOCK, (b + 1) * LRU_BLOCK)
        xv = [stage_ref[b, pl.ds(v, 8, stride=seg), :] for v in range(seg)]
        wrap = [jnp.where(sub == 0,
                          pltpu.roll(halo_ref[b, taps - d], 1, axis=0),
                          pltpu.roll(xv[seg - d], 1, axis=0)) for d in range(1, taps + 1)]
        for d in range(1, taps + 1):
            halo_ref[b, taps - d] = xv[seg - d]

        def src(v):
            return xv[v] if v >= 0 else wrap[-v - 1]

        cw = cw_ref[:, lanes]
        cb = cb_ref[:, lanes]
        xblk = jnp.concatenate(
            [cw[0:1] * src(v - 3) + cw[1:2] * src(v - 2) + cw[2:3] * src(v - 1) + cw[3:4] * src(v) + cb
             for v in range(seg)], axis=0)
        r = _sigmoid(_mxu(xblk, wr_ref[b]) + br_ref[:, lanes])
        gi = _sigmoid(_mxu(xblk, wi_ref[b]) + bi_ref[:, lanes])
        log_a = r * neg_c_softplus[:, lanes]
        a = jnp.exp(log_a)
        u = jnp.sqrt(1.0 - a * a) * (gi * xblk)

        hs, ps = [u[0:8]], [a[0:8]]
        for v in range(1, seg):
            av = a[v * 8:(v + 1) * 8]
            hs.append(av * hs[-1] + u[v * 8:(v + 1) * 8])
            ps.append(av * ps[-1])
        e, pc = hs[-1], ps[-1]
        d = 1
        while d < 8:
            keep = sub >= d
            e = e + pc * jnp.where(keep, pltpu.roll(e, d, axis=0), 0.0)
            pc = pc * jnp.where(keep, pltpu.roll(pc, d, axis=0), 1.0)
            d *= 2
        h0 = h_ref[b, 0:1, :]
        end_state = e + pc * h0
        carry = jnp.where(sub == 0, h0, pltpu.roll(end_state, 1, axis=0))
        h_ref[b, 0:1, :] = end_state[7:8]
        for v in range(seg):
            gate = stage_ref[nb + b, pl.ds(v, 8, stride=seg), :]
            out_ref[b, pl.ds(v, 8, stride=seg), :] = (hs[v] + ps[v] * carry) * _gelu_tanh(gate)
        if b % 2:
            tick()

    tick.flush()
    for b in range(nb):
        o_ref[:, b * LRU_BLOCK:(b + 1) * LRU_BLOCK] = out_ref[b].astype(BF16)


def _lru_mixer(hn, w_in, w_out, j, conv_w, conv_b, w_r, b_r, w_i, b_i, lam, x, g_post, g_next):
    s = hn.shape[0]
    rows = min(LRU_ROWS, s)
    n_steps = s // rows
    w = LRU_WIDTH
    proj, w_out_b = _mm_cols(hn, w_in, j, 2 * w, w_out, j)
    vec = pl.BlockSpec((1, w), lambda i: (0, 0))
    blk = pl.BlockSpec((LRU_BLOCKS, LRU_BLOCK, LRU_BLOCK), lambda i: (0, 0, 0))
    return _chunks_with_out_proj(
        _lru_kernel, n_steps, rows,
        [pl.BlockSpec((rows, 2 * w), lambda i: (jnp.minimum(i, n_steps - 1), 0)),
         pl.BlockSpec((CONV_WIDTH, w), lambda i: (0, 0)), vec, blk, vec, blk, vec, vec],
        (proj, conv_w, conv_b.reshape(1, w), w_r, b_r.reshape(1, w), w_i, b_i.reshape(1, w), lam.reshape(1, w)),
        [pltpu.VMEM((2 * LRU_BLOCKS, rows, LRU_BLOCK), F32),
         pltpu.VMEM((LRU_BLOCKS, rows, LRU_BLOCK), F32),
         pltpu.VMEM((LRU_BLOCKS, CONV_WIDTH - 1, 8, LRU_BLOCK), F32),
         pltpu.VMEM((LRU_BLOCKS, 8, LRU_BLOCK), F32)],
        w_out_b, x, g_post, g_next, "lru_scan")


@jax.jit
def kernel(x, positions, norm_g, mlp_w_up, mlp_w_down, gdn_w_in, gdn_conv_w, gdn_a_log, gdn_dt_bias, gdn_norm_g, gdn_w_out, mlstm_w_in, mlstm_b_i, mlstm_b_f, mlstm_norm_g, mlstm_w_out, ret_w_in, ret_norm_g, ret_w_out, lru_w_in, lru_conv_w, lru_conv_b, lru_w_r, lru_b_r, lru_w_i, lru_b_i, lru_lambda, lru_w_out):
    batch, seq, _ = x.shape
    assert batch == 1
    depth = norm_g.shape[0]
    xs = x.reshape(seq, D_MODEL)
    hn = _prenorm(xs, norm_g[0, 0])
    for layer in range(depth):
        kind, j = layer % 4, layer // 4
        if kind == 0:
            w_out, mixed = _gdn_mixer(hn, gdn_w_in, gdn_w_out, j, gdn_conv_w[j], gdn_a_log[j], gdn_dt_bias[j],
                                      gdn_norm_g[j])
        elif kind == 2:
            w_out, mixed = _ret_mixer(hn, positions, ret_w_in, ret_w_out, j, ret_norm_g[j])
        if kind in (0, 2):
            xs, hn = _mm_rows(mixed, w_out, xs, norm_g[layer, 1], norm_g[layer, 2])
        elif kind == 1:
            xs, hn = _mlstm_mixer(hn, mlstm_w_in, mlstm_w_out, j, mlstm_b_i[j], mlstm_b_f[j], mlstm_norm_g[j],
                                  xs, norm_g[layer, 1], norm_g[layer, 2])
        else:
            xs, hn = _lru_mixer(hn, lru_w_in, lru_w_out, j, lru_conv_w[j], lru_conv_b[j], lru_w_r[j],
                                lru_b_r[j], lru_w_i[j], lru_b_i[j], lru_lambda[j],
                                xs, norm_g[layer, 1], norm_g[layer, 2])
        hmid, w_down = _mm_cols(hn, mlp_w_up, layer, D_FF, mlp_w_down, layer, act="relu2", out_dtype=BF16)
        g_next = norm_g[layer + 1, 0] if layer + 1 < depth else None
        xs, hn = _mm_rows(hmid, w_down, xs, norm_g[layer, 3], g_next)
    return xs.reshape(batch, seq, D_MODEL)
```

```python
import functools
import math

import numpy as np
import jax
import jax.numpy as jnp
from jax import lax
from jax.experimental import pallas as pl
from jax.experimental.pallas import tpu as pltpu

F32 = jnp.float32
BF16 = jnp.bfloat16

D_MODEL = 2048
D_FF = 4 * D_MODEL
NORM_EPS = 1e-6
CONV_WIDTH = 4

GDN_HEADS, GDN_DK, GDN_DV = 16, 128, 128
GDN_CHUNK = 64
MLSTM_HEADS, MLSTM_DQK, MLSTM_DV = 8, 128, 256
MLSTM_CHUNK = 256
GATE_SOFTCAP = 15.0
RET_HEADS, RET_DK, RET_DV = 8, 256, 512
RET_CHUNK = 256
ROPE_BASE = 10000.0
LRU_WIDTH, LRU_BLOCKS, LRU_BLOCK = 2048, 16, 128
LRU_ROWS = 128
LRU_C = 8.0

VMEM_LIMIT_BYTES = 56 * 1024 * 1024

NT_DIMS = (((1,), (1,)), ((), ()))
TN_DIMS = (((0,), (0,)), ((), ()))


def _params(*semantics):
    return pltpu.CompilerParams(dimension_semantics=semantics,
                                vmem_limit_bytes=VMEM_LIMIT_BYTES)


def _rms(x, g):
    y = x * lax.rsqrt(jnp.mean(x * x, axis=-1, keepdims=True) + NORM_EPS)
    return y * g


def _sigmoid(x):
    return 1.0 / (1.0 + jnp.exp(-x))


def _silu(x):
    return x * _sigmoid(x)


def _softplus(x):
    return jnp.maximum(x, 0.0) + jnp.log1p(jnp.exp(-jnp.abs(x)))


def _mxu(a, b, dims=None):
    a = a.astype(BF16)
    b = b.astype(BF16)
    if dims is None:
        return jnp.dot(a, b, preferred_element_type=F32)
    return lax.dot_general(a, b, dims, preferred_element_type=F32)


def _split_bf16(a):
    hi = a.astype(BF16)
    lo = (a - hi.astype(F32)).astype(BF16)
    return hi, lo


def _mxu3(a, b):
    m = a.shape[0]
    a_hi, a_lo = _split_bf16(a)
    b_hi, b_lo = _split_bf16(b)
    top = jnp.dot(jnp.concatenate([a_hi, a_lo], axis=0), b_hi, preferred_element_type=F32)
    return top[:m] + top[m:] + jnp.dot(a_hi, b_lo, preferred_element_type=F32)


def _tri_masks(n):
    row = lax.broadcasted_iota(jnp.int32, (n, n), 0)
    col = lax.broadcasted_iota(jnp.int32, (n, n), 1)
    return row >= col, row > col


def _cumsum_both(col_vals, row_vals, n):
    row = lax.broadcasted_iota(jnp.int32, (n, n), 0)
    col = lax.broadcasted_iota(jnp.int32, (n, n), 1)
    lower = jnp.where(row >= col, 1.0, 0.0).astype(F32)
    upper = jnp.where(row <= col, 1.0, 0.0).astype(F32)
    return _mxu3(lower, col_vals), _mxu3(row_vals, upper)


def _prenorm_kernel(x_ref, g_ref, o_ref):
    o_ref[...] = _rms(x_ref[...], g_ref[...]).astype(o_ref.dtype)


def _prenorm(x, g):
    s = x.shape[0]
    tm = 512
    return pl.pallas_call(
        _prenorm_kernel,
        grid=(s // tm,),
        in_specs=[pl.BlockSpec((tm, D_MODEL), lambda i: (i, 0)),
                  pl.BlockSpec((1, D_MODEL), lambda i: (0, 0))],
        out_specs=pl.BlockSpec((tm, D_MODEL), lambda i: (i, 0)),
        out_shape=jax.ShapeDtypeStruct((s, D_MODEL), BF16),
        compiler_params=_params("parallel"),
        name="prenorm",
    )(x, g.reshape(1, D_MODEL))


def _mm_cols_kernel(x_ref, w_ref, s_ref, o_ref, so_ref, wb_ref, *, act, w_is_transposed):
    @pl.when(pl.program_id(1) == 0)
    def _():
        wb_ref[...] = w_ref[...].astype(BF16)

    if w_is_transposed:
        acc = lax.dot_general(x_ref[...], wb_ref[...], NT_DIMS, preferred_element_type=F32)
    else:
        acc = jnp.dot(x_ref[...], wb_ref[...], preferred_element_type=F32)
    if act == "relu2":
        acc = jnp.square(jnp.maximum(acc, 0.0))
    o_ref[...] = acc.astype(o_ref.dtype)
    so_ref[...] = s_ref[...].astype(BF16)


def _side_rows(n_rows, n_steps):
    rows = 16
    while rows * n_steps < n_rows or n_rows % rows:
        rows += 16
    return rows


def _mm_cols(xb, w, layer, n_cols, side, side_layer, act=None, out_dtype=F32, w_is_transposed=False,
             tm=1024, tn=1024):
    m, k = xb.shape
    tm = min(tm, m)
    ni = m // tm
    nj = n_cols // tn
    _, sr, sc = side.shape
    rb = _side_rows(sr, ni * nj)
    last = sr // rb - 1
    side_idx = lambda j, i: jnp.minimum(j * ni + i, last)
    if w_is_transposed:
        w_spec = pl.BlockSpec((None, tn, k), lambda j, i: (layer, j, 0))
        wb_shape = (tn, k)
    else:
        w_spec = pl.BlockSpec((None, k, tn), lambda j, i: (layer, 0, j))
        wb_shape = (k, tn)
    return pl.pallas_call(
        functools.partial(_mm_cols_kernel, act=act, w_is_transposed=w_is_transposed),
        grid=(nj, ni),
        in_specs=[pl.BlockSpec((tm, k), lambda j, i: (i, 0)),
                  w_spec,
                  pl.BlockSpec((None, rb, sc), lambda j, i: (side_layer, side_idx(j, i), 0))],
        out_specs=[pl.BlockSpec((tm, tn), lambda j, i: (i, j)),
                   pl.BlockSpec((rb, sc), lambda j, i: (side_idx(j, i), 0))],
        out_shape=[jax.ShapeDtypeStruct((m, n_cols), out_dtype),
                   jax.ShapeDtypeStruct((sr, sc), BF16)],
        scratch_shapes=[pltpu.VMEM(wb_shape, BF16)],
        compiler_params=_params("arbitrary", "arbitrary"),
        name="mm_cols",
    )(xb, w, side)


def _tail_kernel(x_ref, w_ref, o_ref):
    o_ref[...] = jnp.dot(x_ref[...], w_ref[...].astype(BF16), preferred_element_type=F32)


def _mm_tail(xb, w_tail):
    m, k = xb.shape
    nt = w_tail.shape[1]
    tm = min(512, m)
    return pl.pallas_call(
        _tail_kernel,
        grid=(m // tm,),
        in_specs=[pl.BlockSpec((tm, k), lambda i: (i, 0)),
                  pl.BlockSpec((k, nt), lambda i: (0, 0))],
        out_specs=pl.BlockSpec((tm, nt), lambda i: (i, 0)),
        out_shape=jax.ShapeDtypeStruct((m, nt), F32),
        compiler_params=_params("parallel"),
        name="mm_tail",
    )(xb, w_tail)


def _chunk_rows(col_vals, n):
    s, nh = col_vals.shape
    return jnp.swapaxes(col_vals.reshape(s // n, n, nh), 1, 2)


def _mm_rows_kernel(a_ref, w_ref, x_ref, gp_ref, gn_ref, *refs, nb, nk, with_next):
    xo_ref = refs[0]
    hn_ref = refs[1] if with_next else None
    accs = refs[-2:]
    i = pl.program_id(0)
    k = pl.program_id(1)
    rows = x_ref.shape[0]

    def matmul(acc_ref):
        acc_ref[...] += jnp.dot(a_ref[...], w_ref[...], preferred_element_type=F32)

    def epilogue(acc_ref):
        rows_k = pl.ds(pl.multiple_of(k * rows, rows), rows)
        acc = acc_ref[rows_k, :]
        acc_ref[rows_k, :] = jnp.zeros_like(acc)
        x_new = x_ref[...] + _rms(acc, gp_ref[...])
        xo_ref[...] = x_new
        if with_next:
            hn_ref[...] = _rms(x_new, gn_ref[...]).astype(BF16)

    @pl.when(jnp.logical_and(i == 0, k == 0))
    def _():
        for acc_ref in accs:
            acc_ref[...] = jnp.zeros_like(acc_ref)

    @pl.when(i == 0)
    def _():
        matmul(accs[0])

    for parity in (0, 1):
        @pl.when(jnp.logical_and(jnp.logical_and(i > 0, i < nb), i % 2 == parity))
        def _():
            epilogue(accs[1 - parity])
            matmul(accs[parity])

    @pl.when(i == nb)
    def _():
        epilogue(accs[(nb - 1) % 2])


def _mm_rows_tiles(m, k):
    if k <= D_MODEL:
        return min(512, m), k
    return min(1024, m), (2048 if k >= 4 * D_MODEL else 1024)


def _mm_rows(a, w, x, g_post, g_next):
    m, k = a.shape
    tm, tk = _mm_rows_tiles(m, k)
    nb, nk = m // tm, k // tk
    rows = tm // nk
    with_next = g_next is not None
    if g_next is None:
        g_next = g_post
    slice_spec = pl.BlockSpec((rows, D_MODEL), lambda i, kk: (jnp.maximum((i - 1) * nk + kk, 0), 0))
    vec_spec = pl.BlockSpec((1, D_MODEL), lambda i, kk: (0, 0))
    out_specs = [slice_spec]
    out_shape = [jax.ShapeDtypeStruct((m, D_MODEL), F32)]
    if with_next:
        out_specs.append(slice_spec)
        out_shape.append(jax.ShapeDtypeStruct((m, D_MODEL), BF16))
    res = pl.pallas_call(
        functools.partial(_mm_rows_kernel, nb=nb, nk=nk, with_next=with_next),
        grid=(nb + 1, nk),
        in_specs=[pl.BlockSpec((tm, tk), lambda i, kk: (jnp.minimum(i, nb - 1), kk)),
                  pl.BlockSpec((tk, D_MODEL), lambda i, kk: (kk, 0)),
                  slice_spec, vec_spec, vec_spec],
        out_specs=out_specs,
        out_shape=out_shape,
        scratch_shapes=[pltpu.VMEM((tm, D_MODEL), F32), pltpu.VMEM((tm, D_MODEL), F32)],
        compiler_params=_params("arbitrary", "arbitrary"),
        name="mm_rows",
    )(a, w, x, g_post.reshape(1, D_MODEL), g_next.reshape(1, D_MODEL))
    return (res[0], res[1]) if with_next else (res[0], None)


OUT_PROJ_PIECES = 8


class _Interleaved:
    def __init__(self, pieces):
        self._pieces = list(pieces)

    def __call__(self):
        if self._pieces:
            self._pieces.pop(0)()

    def flush(self):
        while self._pieces:
            self._pieces.pop(0)()


def _fused_out_proj_kernel(*refs, chunk_body, n_chunk_in):
    chunk_in = refs[:n_chunk_in]
    w_ref, x_ref, gp_ref, gn_ref, xo_ref, hn_ref, mixed_ref, h_ref = refs[n_chunk_in:n_chunk_in + 8]
    chunk_scratch = refs[n_chunk_in + 8:]

    @pl.when(pl.program_id(0) == 0)
    def _():
        mixed_ref[...] = jnp.zeros_like(mixed_ref)

    cols = D_MODEL // OUT_PROJ_PIECES

    def piece(c):
        def run():
            h_ref[:, c * cols:(c + 1) * cols] = jnp.dot(mixed_ref[...], w_ref[:, c * cols:(c + 1) * cols],
                                                        preferred_element_type=F32)
        return run

    chunk_body(*chunk_in, mixed_ref, *chunk_scratch, tick=_Interleaved(piece(c) for c in range(OUT_PROJ_PIECES)))
    x_new = x_ref[...] + _rms(h_ref[...], gp_ref[...])
    xo_ref[...] = x_new
    hn_ref[...] = _rms(x_new, gn_ref[...]).astype(BF16)


def _chunks_with_out_proj(chunk_body, n_steps, rows, chunk_specs, chunk_args, chunk_scratch,
                          w_out, x, g_post, g_next, name):
    k = w_out.shape[0]
    m = x.shape[0]
    trail = pl.BlockSpec((rows, D_MODEL), lambda i: (jnp.maximum(i - 1, 0), 0))
    vec = pl.BlockSpec((1, D_MODEL), lambda i: (0, 0))
    xo, hn = pl.pallas_call(
        functools.partial(_fused_out_proj_kernel, chunk_body=chunk_body, n_chunk_in=len(chunk_specs)),
        grid=(n_steps + 1,),
        in_specs=list(chunk_specs) + [
            pl.BlockSpec((k, D_MODEL), lambda i: (0, 0), pipeline_mode=pl.Buffered(1)), trail, vec, vec],
        out_specs=[trail, trail],
        out_shape=[jax.ShapeDtypeStruct((m, D_MODEL), F32), jax.ShapeDtypeStruct((m, D_MODEL), BF16)],
        scratch_shapes=[pltpu.VMEM((rows, k), BF16), pltpu.VMEM((rows, D_MODEL), F32)] + list(chunk_scratch),
        compiler_params=_params("arbitrary"),
        name=name,
    )(*chunk_args, w_out, x, g_post.reshape(1, D_MODEL), g_next.reshape(1, D_MODEL))
    return xo, hn


def _segment_conv(plane_ref, halo_plane, cw, seg):
    taps = CONV_WIDTH - 1
    sub = lax.broadcasted_iota(jnp.int32, (8, plane_ref.shape[-1]), 0)
    xv = [plane_ref[pl.ds(v, 8, stride=seg), :] for v in range(seg)]
    wrap = [jnp.where(sub == 0,
                      pltpu.roll(halo_plane[taps - d], 1, axis=0),
                      pltpu.roll(xv[seg - d], 1, axis=0)) for d in range(1, taps + 1)]
    for d in range(1, taps + 1):
        halo_plane[taps - d] = xv[seg - d]

    def src(v):
        return xv[v] if v >= 0 else wrap[-v - 1]

    return [cw[0:1] * src(v - 3) + cw[1:2] * src(v - 2) + cw[2:3] * src(v - 1) + cw[3:4] * src(v)
            for v in range(seg)]


GDN_HEAD_GROUP = 16


def _unit_lower_inverses(mats):
    n = mats[0].shape[0]
    row = lax.broadcasted_iota(jnp.int32, (n, n), 0)
    col = lax.broadcasted_iota(jnp.int32, (n, n), 1)
    eye = jnp.where(row == col, 1.0, 0.0).astype(F32)

    def same_block(b):
        shift = int(math.log2(b))
        return (row >> shift) == (col >> shift)

    b = 2
    ds = [eye - jnp.where(same_block(b), a, 0.0) for a in mats]
    while b < n:
        newly = jnp.logical_and(same_block(2 * b), jnp.logical_not(same_block(b)))
        es = [jnp.where(newly, a, 0.0) for a in mats]
        eds = [_mxu(e, d) for e, d in zip(es, ds)]
        ds = [d - _mxu(d, ed) for d, ed in zip(ds, eds)]
        b *= 2
    return ds


def _gdn_kernel(qkv_ref, z_ref, b_ref, a_ref, at_ref, cw_ref, al_ref, dtb_ref, alt_ref, dtbt_ref, ng_ref,
                o_ref, halo_ref, stage_ref, act_ref, s_ref):
    n = GDN_CHUNK
    nh = GDN_HEADS
    seg = n // 8
    planes = stage_ref.shape[0]
    plane_w = stage_ref.shape[-1]

    @pl.when(pl.program_id(0) == 0)
    def _():
        s_ref[...] = jnp.zeros_like(s_ref)
        halo_ref[...] = jnp.zeros_like(halo_ref)

    for c in range(planes):
        stage_ref[c] = qkv_ref[:, c * plane_w:(c + 1) * plane_w]
    for c in range(planes):
        conv = _segment_conv(stage_ref.at[c], halo_ref.at[c], cw_ref[:, c * plane_w:(c + 1) * plane_w], seg)
        for v in range(seg):
            act_ref[c, pl.ds(v, 8, stride=seg), :] = _silu(conv[v])

    beta = _sigmoid(b_ref[...])
    g_col = -jnp.exp(al_ref[...]) * _softplus(a_ref[...] + dtb_ref[...])
    g_row = -jnp.exp(alt_ref[...]) * _softplus(at_ref[...] + dtbt_ref[...])
    gc_col, gc_row = _cumsum_both(g_col, g_row, n)
    incl, strict = _tri_masks(n)
    ng = ng_ref[...]
    g_last = gc_col[n - 1:n, :]
    eg_all = jnp.exp(gc_col)
    beg_all = beta * eg_all
    kdec_all = jnp.exp(g_last - gc_col)
    sdec_all = jnp.exp(g_last)

    def l2norm(t):
        return t * lax.rsqrt(jnp.sum(t * t, axis=-1, keepdims=True) + NORM_EPS)

    for g0 in range(0, GDN_HEADS, GDN_HEAD_GROUP):
        heads = range(g0, g0 + GDN_HEAD_GROUP)
        q = [l2norm(act_ref[h]) * GDN_DK ** -0.5 for h in heads]
        k = [l2norm(act_ref[nh + h]) for h in heads]
        v = [act_ref[2 * nh + h] for h in heads]
        gcc = [gc_col[:, h:h + 1] for h in heads]
        gcr = [gc_row[h:h + 1, :] for h in heads]
        bc = [beta[:, h:h + 1] for h in heads]
        eg = [eg_all[:, h:h + 1] for h in heads]
        dec = [jnp.where(incl, jnp.exp(c - r), 0.0) for c, r in zip(gcc, gcr)]
        prod = [_mxu(jnp.concatenate([ki, qi], axis=0), ki, NT_DIMS) for ki, qi in zip(k, q)]
        a_mat = [jnp.where(strict, b * p[:n] * d, 0.0) for b, p, d in zip(bc, prod, dec)]
        t_inv = _unit_lower_inverses(a_mat)
        rhs = [jnp.concatenate([vi * b, ki * beg_all[:, h:h + 1]], axis=1)
               for vi, ki, b, h in zip(v, k, bc, heads)]
        uw = [r + _mxu(jnp.where(strict, t, 0.0), r) for t, r in zip(t_inv, rhs)]
        state = [s_ref[h] for h in heads]
        qs_ws = [_mxu(jnp.concatenate([qi * e, x[:, GDN_DV:]], axis=0), st)
                 for qi, e, x, st in zip(q, eg, uw, state)]
        v_new = [x[:, :GDN_DV] - y[n:] for x, y in zip(uw, qs_ws)]
        o = [y[:n] + _mxu(p[n:] * d, vn) for y, p, d, vn in zip(qs_ws, prod, dec, v_new)]
        upd = [_mxu(ki * kdec_all[:, h:h + 1], vn, TN_DIMS) for ki, h, vn in zip(k, heads, v_new)]
        for i, h in enumerate(heads):
            s_ref[h] = state[i] * sdec_all[:, h:h + 1] + upd[i]
            zh = z_ref[:, h * GDN_DV:(h + 1) * GDN_DV]
            o_ref[:, h * GDN_DV:(h + 1) * GDN_DV] = (_rms(o[i], ng) * _silu(zh)).astype(BF16)


def _gdn_mixer(hn, w_in, w_out, j, conv_w, a_log, dt_bias, norm_g):
    s = hn.shape[0]
    n = GDN_CHUNK
    nh = GDN_HEADS
    c_qkv = 2 * nh * GDN_DK + nh * GDN_DV
    c_z = nh * GDN_DV
    proj, w_out_b = _mm_cols(hn, jnp.swapaxes(w_in, 1, 2), j, c_qkv + c_z, w_out, j, w_is_transposed=True)
    tail = _mm_tail(hn, w_in[j, :, c_qkv + c_z:])
    b_pre, a_pre = tail[:, :nh], tail[:, nh:]
    a_pre_t = _chunk_rows(a_pre, n)
    full = lambda shape: pl.BlockSpec(shape, lambda i: (0, 0))
    return w_out_b, pl.pallas_call(
        _gdn_kernel,
        grid=(s // n,),
        in_specs=[pl.BlockSpec((n, c_qkv), lambda i: (i, 0)),
                  pl.BlockSpec((n, c_z), lambda i: (i, c_qkv // c_z)),
                  pl.BlockSpec((n, nh), lambda i: (i, 0)),
                  pl.BlockSpec((n, nh), lambda i: (i, 0)),
                  pl.BlockSpec((None, nh, n), lambda i: (i, 0, 0)),
                  full((CONV_WIDTH, c_qkv)),
                  full((1, nh)), full((1, nh)), full((nh, 1)), full((nh, 1)),
                  full((1, GDN_DV))],
        out_specs=pl.BlockSpec((n, c_z), lambda i: (i, 0)),
        out_shape=jax.ShapeDtypeStruct((s, c_z), BF16),
        scratch_shapes=[pltpu.VMEM((c_qkv // GDN_DK, CONV_WIDTH - 1, 8, GDN_DK), F32),
                        pltpu.VMEM((c_qkv // GDN_DK, n, GDN_DK), F32),
                        pltpu.VMEM((c_qkv // GDN_DK, n, GDN_DK), F32),
                        pltpu.VMEM((nh, GDN_DK, GDN_DV), F32)],
        compiler_params=_params("arbitrary"),
        name="gdn_chunks",
    )(proj, proj, b_pre, a_pre, a_pre_t, conv_w,
      a_log.reshape(1, nh), dt_bias.reshape(1, nh), a_log.reshape(nh, 1), dt_bias.reshape(nh, 1),
      norm_g.reshape(1, GDN_DV))


def _softcap(t):
    return GATE_SOFTCAP * jnp.tanh(t / GATE_SOFTCAP)


def _mlstm_kernel(p_ref, i_ref, f_ref, it_ref, ft_ref, bi_ref, bf_ref, bit_ref, bft_ref, ng_ref,
                  o_ref, c_ref, n_ref, m_ref, *, tick):
    n = MLSTM_CHUNK
    nh, dqk, dv = MLSTM_HEADS, MLSTM_DQK, MLSTM_DV

    @pl.when(pl.program_id(0) == 0)
    def _():
        c_ref[...] = jnp.zeros_like(c_ref)
        n_ref[...] = jnp.zeros_like(n_ref)
        m_ref[...] = jnp.zeros_like(m_ref)

    i_col = _softcap(i_ref[...] + bi_ref[...])
    i_row = _softcap(it_ref[...] + bit_ref[...])
    f_col = -_softplus(-_softcap(f_ref[...] + bf_ref[...]))
    f_row = -_softplus(-_softcap(ft_ref[...] + bft_ref[...]))
    b_col, b_row = _cumsum_both(f_col, f_row, n)
    incl, _ = _tri_masks(n)

    heads = range(nh)
    q = [p_ref[:, h * dqk:(h + 1) * dqk] * dqk ** -0.5 for h in heads]
    k = [p_ref[:, nh * dqk + h * dqk:nh * dqk + (h + 1) * dqk] for h in heads]
    v = [p_ref[:, 2 * nh * dqk + h * dv:2 * nh * dqk + (h + 1) * dv] for h in heads]
    bcc = [b_col[:, h:h + 1] for h in heads]
    bcr = [b_row[h:h + 1, :] for h in heads]
    icc = [i_col[:, h:h + 1] for h in heads]
    icr = [i_row[h:h + 1, :] for h in heads]
    m_prev = [m_ref[h:h + 1, 0:1] for h in heads]
    qk = [_mxu(qi, ki, NT_DIMS) for qi, ki in zip(q, k)]
    c_mem = [c_ref[h] for h in heads]
    n_mem = [n_ref[h:h + 1, :] for h in heads]
    qc = [_mxu(qi, c) for qi, c in zip(q, c_mem)]
    d_log = [jnp.where(incl, c - r + i, -jnp.inf) for c, r, i in zip(bcc, bcr, icr)]
    tick()
    inter = [c + m for c, m in zip(bcc, m_prev)]
    m_t = [jnp.maximum(x, jnp.max(d, axis=-1, keepdims=True)) for x, d in zip(inter, d_log)]
    sc = [s * jnp.exp(d - m) for s, d, m in zip(qk, d_log, m_t)]
    tick()
    w_inter = [jnp.exp(x - m) for x, m in zip(inter, m_t)]
    num = [w * x + _mxu(s, vi) for w, x, s, vi in zip(w_inter, qc, sc, v)]
    tick()
    den = [w * jnp.sum(qi * nm, axis=-1, keepdims=True) + jnp.sum(s, axis=-1, keepdims=True)
           for w, qi, nm, s in zip(w_inter, q, n_mem, sc)]
    hid = [x / jnp.maximum(jnp.abs(d), jnp.exp(-m)) for x, d, m in zip(num, den, m_t)]
    tick()
    b_last = [r[:, n - 1:n] for r in bcr]
    end_max = [jnp.max(bl - r + i, axis=-1, keepdims=True) for bl, r, i in zip(b_last, bcr, icr)]
    m_new = [jnp.maximum(bl + m, e) for bl, m, e in zip(b_last, m_prev, end_max)]
    decay = [jnp.exp(bl + m - mn) for bl, m, mn in zip(b_last, m_prev, m_new)]
    k_w = [ki * jnp.exp(bl - c + i - mn) for ki, bl, c, i, mn in zip(k, b_last, bcc, icc, m_new)]
    upd = [_mxu(kw, vi, TN_DIMS) for kw, vi in zip(k_w, v)]
    outs = []
    for h in heads:
        c_ref[h] = decay[h] * c_mem[h] + upd[h]
        n_ref[h:h + 1, :] = decay[h] * n_mem[h] + jnp.sum(k_w[h], axis=0, keepdims=True)
        m_ref[h:h + 1, :] = jnp.broadcast_to(m_new[h], (1, m_ref.shape[1]))
        o_pre = p_ref[:, 2 * nh * dqk + nh * dv + h * dv:2 * nh * dqk + nh * dv + (h + 1) * dv]
        normed = _rms(hid[h], ng_ref[:, h * dv:(h + 1) * dv])
        outs.append((normed * _sigmoid(o_pre)).astype(BF16))
        tick()
    tick.flush()
    for h in heads:
        o_ref[:, h * dv:(h + 1) * dv] = outs[h]


def _mlstm_mixer(hn, w_in, w_out, j, b_i, b_f, norm_g, x, g_post, g_next):
    s = hn.shape[0]
    n = min(MLSTM_CHUNK, s)
    n_steps = s // n
    nh, dqk, dv = MLSTM_HEADS, MLSTM_DQK, MLSTM_DV
    c_main = 2 * nh * dqk + 2 * nh * dv
    proj, w_out_b = _mm_cols(hn, jnp.swapaxes(w_in, 1, 2), j, c_main, w_out, j, w_is_transposed=True)
    tail = _mm_tail(hn, w_in[j, :, c_main:])
    i_pre, f_pre = tail[:, :nh], tail[:, nh:]
    step = lambda i: jnp.minimum(i, n_steps - 1)
    full = lambda shape: pl.BlockSpec(shape, lambda i: (0, 0))
    col = pl.BlockSpec((n, nh), lambda i: (step(i), 0))
    row = pl.BlockSpec((None, nh, n), lambda i: (step(i), 0, 0))
    return _chunks_with_out_proj(
        _mlstm_kernel, n_steps, n,
        [pl.BlockSpec((n, c_main), lambda i: (step(i), 0)), col, col, row, row,
         full((1, nh)), full((1, nh)), full((nh, 1)), full((nh, 1)), full((1, nh * dv))],
        (proj, i_pre, f_pre, _chunk_rows(i_pre, n), _chunk_rows(f_pre, n),
         b_i.reshape(1, nh), b_f.reshape(1, nh), b_i.reshape(nh, 1), b_f.reshape(nh, 1),
         norm_g.reshape(1, nh * dv)),
        [pltpu.VMEM((nh, dqk, dv), F32), pltpu.VMEM((nh, dqk), F32), pltpu.VMEM((nh, 128), F32)],
        w_out_b, x, g_post, g_next, "mlstm_chunks")


def _ret_kernel(p_ref, pos_ref, fr_ref, ng_ref, o_ref, r_ref):
    n = RET_CHUNK
    nh, dk, dv = RET_HEADS, RET_DK, RET_DV
    half = dk // 2

    @pl.when(pl.program_id(0) == 0)
    def _():
        r_ref[...] = jnp.zeros_like(r_ref)

    ang = pos_ref[...].astype(F32) * fr_ref[...]
    cos, sin = jnp.cos(ang), jnp.sin(ang)
    row = lax.broadcasted_iota(jnp.int32, (n, n), 0)
    col = lax.broadcasted_iota(jnp.int32, (n, n), 1)
    lag = (row - col).astype(F32)
    t_col = lax.broadcasted_iota(jnp.int32, (n, 1), 0).astype(F32)

    def rope(t):
        t1, t2 = t[:, :half], t[:, half:]
        return jnp.concatenate([t1 * cos - t2 * sin, t1 * sin + t2 * cos], axis=1)

    heads = range(nh)
    log_gamma = [float(np.log1p(-np.exp2(np.float32(-5.0 - h)))) for h in heads]
    q = [rope(p_ref[:, h * dk:(h + 1) * dk]) for h in heads]
    k = [rope(p_ref[:, nh * dk + h * dk:nh * dk + (h + 1) * dk]) * dk ** -0.5 for h in heads]
    v = [p_ref[:, 2 * nh * dk + h * dv:2 * nh * dk + (h + 1) * dv] for h in heads]
    qk = [_mxu(qi, ki, NT_DIMS) for qi, ki in zip(q, k)]
    mem = [r_ref[h] for h in heads]
    qr = [_mxu(qi, m) for qi, m in zip(q, mem)]
    d_mat = [jnp.where(row >= col, jnp.exp(lag * lg), 0.0) for lg in log_gamma]
    intra = [_mxu(s * d, vi) for s, d, vi in zip(qk, d_mat, v)]
    upd = [_mxu(ki * jnp.exp((n - 1.0 - t_col) * lg), vi, TN_DIMS) for ki, lg, vi in zip(k, log_gamma, v)]
    for h in heads:
        o = intra[h] + qr[h] * jnp.exp((t_col + 1.0) * log_gamma[h])
        r_ref[h] = mem[h] * math.exp(n * log_gamma[h]) + upd[h]
        gate = p_ref[:, 2 * nh * dk + nh * dv + h * dv:2 * nh * dk + nh * dv + (h + 1) * dv]
        o = _rms(o, ng_ref[:, h * dv:(h + 1) * dv])
        o_ref[:, h * dv:(h + 1) * dv] = (o * _silu(gate)).astype(BF16)


def _ret_mixer(hn, positions, w_in, w_out, j, norm_g):
    s = hn.shape[0]
    n = min(RET_CHUNK, s)
    nh, dk, dv = RET_HEADS, RET_DK, RET_DV
    c_all = 2 * nh * dk + 2 * nh * dv
    proj, w_out_b = _mm_cols(hn, w_in, j, c_all, w_out, j)
    freqs = ROPE_BASE ** (-jnp.arange(0, dk, 2, dtype=F32) / dk)
    return w_out_b, pl.pallas_call(
        _ret_kernel,
        grid=(s // n,),
        in_specs=[pl.BlockSpec((n, c_all), lambda i: (i, 0)),
                  pl.BlockSpec((n, 1), lambda i: (i, 0)),
                  pl.BlockSpec((1, dk // 2), lambda i: (0, 0)),
                  pl.BlockSpec((1, nh * dv), lambda i: (0, 0))],
        out_specs=pl.BlockSpec((n, nh * dv), lambda i: (i, 0)),
        out_shape=jax.ShapeDtypeStruct((s, nh * dv), BF16),
        scratch_shapes=[pltpu.VMEM((nh, dk, dv), F32)],
        compiler_params=_params("arbitrary"),
        name="ret_chunks",
    )(proj, positions.reshape(s, 1), freqs.reshape(1, dk // 2), norm_g.reshape(1, nh * dv))


def _gelu_tanh(x):
    return 0.5 * x * (1.0 + jnp.tanh(math.sqrt(2.0 / math.pi) * (x + 0.044715 * (x * x * x))))


def _lru_kernel(p_ref, cw_ref, cb_ref, wr_ref, br_ref, wi_ref, bi_ref, lam_ref, o_ref,
                stage_ref, out_ref, halo_ref, h_ref, *, tick):
    rows = p_ref.shape[0]
    seg = rows // 8
    nb = LRU_BLOCKS
    first = pl.program_id(0) == 0

    @pl.when(first)
    def _():
        h_ref[...] = jnp.zeros_like(h_ref)
        halo_ref[...] = jnp.zeros_like(halo_ref)

    for c in range(2 * nb):
        stage_ref[c] = p_ref[:, c * LRU_BLOCK:(c + 1) * LRU_BLOCK]

    sub = lax.broadcasted_iota(jnp.int32, (8, LRU_BLOCK), 0)
    neg_c_softplus = -LRU_C * _softplus(-lam_ref[...])
    taps = CONV_WIDTH - 1

    for b in range(nb):
        lanes = slice(b * LRU_BLOCK, (b + 1) * LRU_BLOCK)
        xv = [stage_ref[b, pl.ds(v, 8, stride=seg), :] for v in range(seg)]
        wrap = [jnp.where(sub == 0,
                          pltpu.roll(halo_ref[b, taps - d], 1, axis=0),
                          pltpu.roll(xv[seg - d], 1, axis=0)) for d in range(1, taps + 1)]
        for d in range(1, taps + 1):
            halo_ref[b, taps - d] = xv[seg - d]

        def src(v):
            return xv[v] if v >= 0 else wrap[-v - 1]

        cw = cw_ref[:, lanes]
        cb = cb_ref[:, lanes]
        xblk = jnp.concatenate(
            [cw[0:1] * src(v - 3) + cw[1:2] * src(v - 2) + cw[2:3] * src(v - 1) + cw[3:4] * src(v) + cb
             for v in range(seg)], axis=0)
        r = _sigmoid(_mxu(xblk, wr_ref[b]) + br_ref[:, lanes])
        gi = _sigmoid(_mxu(xblk, wi_ref[b]) + bi_ref[:, lanes])
        log_a = r * neg_c_softplus[:, lanes]
        a = jnp.exp(log_a)
        u = jnp.sqrt(1.0 - a * a) * (gi * xblk)

        hs, ps = [u[0:8]], [a[0:8]]
        for v in range(1, seg):
            av = a[v * 8:(v + 1) * 8]
            hs.append(av * hs[-1] + u[v * 8:(v + 1) * 8])
            ps.append(av * ps[-1])
        e, pc = hs[-1], ps[-1]
        d = 1
        while d < 8:
            keep = sub >= d
            e = e + pc * jnp.where(keep, pltpu.roll(e, d, axis=0), 0.0)
            pc = pc * jnp.where(keep, pltpu.roll(pc, d, axis=0), 1.0)
            d *= 2
        h0 = h_ref[b, 0:1, :]
        end_state = e + pc * h0
        carry = jnp.where(sub == 0, h0, pltpu.roll(end_state, 1, axis=0))
        h_ref[b, 0:1, :] = end_state[7:8]
        for v in range(seg):
            gate = stage_ref[nb + b, pl.ds(v, 8, stride=seg), :]
            out_ref[b, pl.ds(v, 8, stride=seg), :] = (hs[v] + ps[v] * carry) * _gelu_tanh(gate)
        if b % 2:
            tick()

    tick.flush()
    for b in range(nb):
        o_ref[:, b * LRU_BLOCK:(b + 1) * LRU_BLOCK] = out_ref[b].astype(BF16)


def _lru_mixer(hn, w_in, w_out, j, conv_w, conv_b, w_r, b_r, w_i, b_i, lam, x, g_post, g_next):
    s = hn.shape[0]
    rows = min(LRU_ROWS, s)
    n_steps = s // rows
    w = LRU_WIDTH
    proj, w_out_b = _mm_cols(hn, w_in, j, 2 * w, w_out, j)
    vec = pl.BlockSpec((1, w), lambda i: (0, 0))
    blk = pl.BlockSpec((LRU_BLOCKS, LRU_BLOCK, LRU_BLOCK), lambda i: (0, 0, 0))
    return _chunks_with_out_proj(
        _lru_kernel, n_steps, rows,
        [pl.BlockSpec((rows, 2 * w), lambda i: (jnp.minimum(i, n_steps - 1), 0)),
         pl.BlockSpec((CONV_WIDTH, w), lambda i: (0, 0)), vec, blk, vec, blk, vec, vec],
        (proj, conv_w, conv_b.reshape(1, w), w_r, b_r.reshape(1, w), w_i, b_i.reshape(1, w), lam.reshape(1, w)),
        [pltpu.VMEM((2 * LRU_BLOCKS, rows, LRU_BLOCK), F32),
         pltpu.VMEM((LRU_BLOCKS, rows, LRU_BLOCK), F32),
         pltpu.VMEM((LRU_BLOCKS, CONV_WIDTH - 1, 8, LRU_BLOCK), F32),
         pltpu.VMEM((LRU_BLOCKS, 8, LRU_BLOCK), F32)],
        w_out_b, x, g_post, g_next, "lru_scan")


@jax.jit
def kernel(x, positions, norm_g, mlp_w_up, mlp_w_down, gdn_w_in, gdn_conv_w, gdn_a_log, gdn_dt_bias, gdn_norm_g, gdn_w_out, mlstm_w_in, mlstm_b_i, mlstm_b_f, mlstm_norm_g, mlstm_w_out, ret_w_in, ret_norm_g, ret_w_out, lru_w_in, lru_conv_w, lru_conv_b, lru_w_r, lru_b_r, lru_w_i, lru_b_i, lru_lambda, lru_w_out):
    batch, seq, _ = x.shape
    assert batch == 1
    depth = norm_g.shape[0]
    xs = x.reshape(seq, D_MODEL)
    hn = _prenorm(xs, norm_g[0, 0])
    for layer in range(depth):
        kind, j = layer % 4, layer // 4
        if kind == 0:
            w_out, mixed = _gdn_mixer(hn, gdn_w_in, gdn_w_out, j, gdn_conv_w[j], gdn_a_log[j], gdn_dt_bias[j],
                                      gdn_norm_g[j])
            xs, hn = _mm_rows(mixed, w_out, xs, norm_g[layer, 1], norm_g[layer, 2])
        elif kind == 2:
            w_out, mixed = _ret_mixer(hn, positions, ret_w_in, ret_w_out, j, ret_norm_g[j])
            xs, hn = _mm_rows(mixed, w_out, xs, norm_g[layer, 1], norm_g[layer, 2])
        elif kind == 1:
            xs, hn = _mlstm_mixer(hn, mlstm_w_in, mlstm_w_out, j, mlstm_b_i[j], mlstm_b_f[j], mlstm_norm_g[j],
                                  xs, norm_g[layer, 1], norm_g[layer, 2])
        else:
            xs, hn = _lru_mixer(hn, lru_w_in, lru_w_out, j, lru_conv_w[j], lru_conv_b[j], lru_w_r[j],
                                lru_b_r[j], lru_w_i[j], lru_b_i[j], lru_lambda[j],
                                xs, norm_g[layer, 1], norm_g[layer, 2])
        hmid, w_down = _mm_cols(hn, mlp_w_up, layer, D_FF, mlp_w_down, layer, act="relu2", out_dtype=BF16)
        g_next = norm_g[layer + 1, 0] if layer + 1 < depth else None
        xs, hn = _mm_rows(hmid, w_down, xs, norm_g[layer, 3], g_next)
    return xs.reshape(batch, seq, D_MODEL)
```

```python
import functools
import math

import numpy as np
import jax
import jax.numpy as jnp
from jax import lax
from jax.experimental import pallas as pl
from jax.experimental.pallas import tpu as pltpu

F32 = jnp.float32
BF16 = jnp.bfloat16

D_MODEL = 2048
D_FF = 4 * D_MODEL
NORM_EPS = 1e-6
CONV_WIDTH = 4

GDN_HEADS, GDN_DK, GDN_DV = 16, 128, 128
GDN_CHUNK = 128
MLSTM_HEADS, MLSTM_DQK, MLSTM_DV = 8, 128, 256
MLSTM_CHUNK = 256
GATE_SOFTCAP = 15.0
RET_HEADS, RET_DK, RET_DV = 8, 256, 512
RET_CHUNK = 256
ROPE_BASE = 10000.0
LRU_WIDTH, LRU_BLOCKS, LRU_BLOCK = 2048, 16, 128
LRU_ROWS = 128
LRU_C = 8.0

VMEM_LIMIT_BYTES = 56 * 1024 * 1024

NT_DIMS = (((1,), (1,)), ((), ()))
TN_DIMS = (((0,), (0,)), ((), ()))


def _params(*semantics):
    return pltpu.CompilerParams(dimension_semantics=semantics,
                                vmem_limit_bytes=VMEM_LIMIT_BYTES)


def _rms(x, g):
    y = x * lax.rsqrt(jnp.mean(x * x, axis=-1, keepdims=True) + NORM_EPS)
    return y * g


def _sigmoid(x):
    return 1.0 / (1.0 + jnp.exp(-x))


def _silu(x):
    return x * _sigmoid(x)


def _softplus(x):
    return jnp.maximum(x, 0.0) + jnp.log1p(jnp.exp(-jnp.abs(x)))


def _mxu(a, b, dims=None):
    a = a.astype(BF16)
    b = b.astype(BF16)
    if dims is None:
        return jnp.dot(a, b, preferred_element_type=F32)
    return lax.dot_general(a, b, dims, preferred_element_type=F32)


def _split3(a):
    hi = a.astype(BF16).astype(F32)
    rest = a - hi
    mid = rest.astype(BF16).astype(F32)
    return hi, mid, (rest - mid).astype(BF16).astype(F32)


def _tri_masks(n):
    row = lax.broadcasted_iota(jnp.int32, (n, n), 0)
    col = lax.broadcasted_iota(jnp.int32, (n, n), 1)
    return row >= col, row > col


def _cumsum_both(col_vals, row_vals, n):
    h = col_vals.shape[1]
    row = lax.broadcasted_iota(jnp.int32, (n, n), 0)
    col = lax.broadcasted_iota(jnp.int32, (n, n), 1)
    lower = jnp.where(row >= col, 1.0, 0.0).astype(BF16)
    upper = jnp.where(row <= col, 1.0, 0.0).astype(BF16)
    cs = jnp.dot(lower, jnp.concatenate(_split3(col_vals), axis=1).astype(BF16), preferred_element_type=F32)
    rs = jnp.dot(jnp.concatenate(_split3(row_vals), axis=0).astype(BF16), upper, preferred_element_type=F32)
    return (cs[:, 2 * h:] + cs[:, h:2 * h]) + cs[:, :h], (rs[2 * h:] + rs[h:2 * h]) + rs[:h]


def _prenorm_kernel(x_ref, g_ref, o_ref):
    o_ref[...] = _rms(x_ref[...], g_ref[...]).astype(o_ref.dtype)


def _prenorm(x, g):
    s = x.shape[0]
    tm = 512
    return pl.pallas_call(
        _prenorm_kernel,
        grid=(s // tm,),
        in_specs=[pl.BlockSpec((tm, D_MODEL), lambda i: (i, 0)),
                  pl.BlockSpec((1, D_MODEL), lambda i: (0, 0))],
        out_specs=pl.BlockSpec((tm, D_MODEL), lambda i: (i, 0)),
        out_shape=jax.ShapeDtypeStruct((s, D_MODEL), BF16),
        compiler_params=_params("parallel"),
        name="prenorm",
    )(x, g.reshape(1, D_MODEL))


def _mm_cols_kernel(x_ref, w_ref, s_ref, o_ref, so_ref, wb_ref, *, act, w_is_transposed):
    @pl.when(pl.program_id(1) == 0)
    def _():
        wb_ref[...] = w_ref[...].astype(BF16)

    if w_is_transposed:
        acc = lax.dot_general(x_ref[...], wb_ref[...], NT_DIMS, preferred_element_type=F32)
    else:
        acc = jnp.dot(x_ref[...], wb_ref[...], preferred_element_type=F32)
    if act == "relu2":
        acc = jnp.square(jnp.maximum(acc, 0.0))
    o_ref[...] = acc.astype(o_ref.dtype)
    so_ref[...] = s_ref[...].astype(BF16)


def _side_rows(n_rows, n_steps):
    rows = 16
    while rows * n_steps < n_rows or n_rows % rows:
        rows += 16
    return rows


def _mm_cols(xb, w, layer, n_cols, side, side_layer, act=None, out_dtype=F32, w_is_transposed=False,
             tm=1024, tn=1024):
    m, k = xb.shape
    tm = min(tm, m)
    ni = m // tm
    nj = n_cols // tn
    _, sr, sc = side.shape
    rb = _side_rows(sr, ni * nj)
    last = sr // rb - 1
    side_idx = lambda j, i: jnp.minimum(j * ni + i, last)
    if w_is_transposed:
        w_spec = pl.BlockSpec((None, tn, k), lambda j, i: (layer, j, 0))
        wb_shape = (tn, k)
    else:
        w_spec = pl.BlockSpec((None, k, tn), lambda j, i: (layer, 0, j))
        wb_shape = (k, tn)
    return pl.pallas_call(
        functools.partial(_mm_cols_kernel, act=act, w_is_transposed=w_is_transposed),
        grid=(nj, ni),
        in_specs=[pl.BlockSpec((tm, k), lambda j, i: (i, 0)),
                  w_spec,
                  pl.BlockSpec((None, rb, sc), lambda j, i: (side_layer, side_idx(j, i), 0))],
        out_specs=[pl.BlockSpec((tm, tn), lambda j, i: (i, j)),
                   pl.BlockSpec((rb, sc), lambda j, i: (side_idx(j, i), 0))],
        out_shape=[jax.ShapeDtypeStruct((m, n_cols), out_dtype),
                   jax.ShapeDtypeStruct((sr, sc), BF16)],
        scratch_shapes=[pltpu.VMEM(wb_shape, BF16)],
        compiler_params=_params("arbitrary", "arbitrary"),
        name="mm_cols",
    )(xb, w, side)


def _tail_kernel(x_ref, w_ref, o_ref):
    o_ref[...] = jnp.dot(x_ref[...], w_ref[...].astype(BF16), preferred_element_type=F32)


def _mm_tail(xb, w_tail):
    m, k = xb.shape
    nt = w_tail.shape[1]
    tm = min(512, m)
    return pl.pallas_call(
        _tail_kernel,
        grid=(m // tm,),
        in_specs=[pl.BlockSpec((tm, k), lambda i: (i, 0)),
                  pl.BlockSpec((k, nt), lambda i: (0, 0))],
        out_specs=pl.BlockSpec((tm, nt), lambda i: (i, 0)),
        out_shape=jax.ShapeDtypeStruct((m, nt), F32),
        compiler_params=_params("parallel"),
        name="mm_tail",
    )(xb, w_tail)


def _chunk_rows(col_vals, n):
    s, nh = col_vals.shape
    return jnp.swapaxes(col_vals.reshape(s // n, n, nh), 1, 2)


def _mm_rows_kernel(a_ref, w_ref, x_ref, gp_ref, gn_ref, *refs, nb, nk, with_next):
    xo_ref = refs[0]
    hn_ref = refs[1] if with_next else None
    accs = refs[-2:]
    i = pl.program_id(0)
    k = pl.program_id(1)
    rows = x_ref.shape[0]

    def matmul(acc_ref):
        acc_ref[...] += jnp.dot(a_ref[...], w_ref[...], preferred_element_type=F32)

    def epilogue(acc_ref):
        rows_k = pl.ds(pl.multiple_of(k * rows, rows), rows)
        acc = acc_ref[rows_k, :]
        acc_ref[rows_k, :] = jnp.zeros_like(acc)
        x_new = x_ref[...] + _rms(acc, gp_ref[...])
        xo_ref[...] = x_new
        if with_next:
            hn_ref[...] = _rms(x_new, gn_ref[...]).astype(BF16)

    @pl.when(jnp.logical_and(i == 0, k == 0))
    def _():
        for acc_ref in accs:
            acc_ref[...] = jnp.zeros_like(acc_ref)

    @pl.when(i == 0)
    def _():
        matmul(accs[0])

    for parity in (0, 1):
        @pl.when(jnp.logical_and(jnp.logical_and(i > 0, i < nb), i % 2 == parity))
        def _():
            epilogue(accs[1 - parity])
            matmul(accs[parity])

    @pl.when(i == nb)
    def _():
        epilogue(accs[(nb - 1) % 2])


def _mm_rows_tiles(m, k):
    if k <= D_MODEL:
        return min(512, m), k
    return min(1024, m), (2048 if k >= 4 * D_MODEL else 1024)


def _mm_rows(a, w, x, g_post, g_next):
    m, k = a.shape
    tm, tk = _mm_rows_tiles(m, k)
    nb, nk = m // tm, k // tk
    rows = tm // nk
    with_next = g_next is not None
    if g_next is None:
        g_next = g_post
    slice_spec = pl.BlockSpec((rows, D_MODEL), lambda i, kk: (jnp.maximum((i - 1) * nk + kk, 0), 0))
    vec_spec = pl.BlockSpec((1, D_MODEL), lambda i, kk: (0, 0))
    out_specs = [slice_spec]
    out_shape = [jax.ShapeDtypeStruct((m, D_MODEL), F32)]
    if with_next:
        out_specs.append(slice_spec)
        out_shape.append(jax.ShapeDtypeStruct((m, D_MODEL), BF16))
    res = pl.pallas_call(
        functools.partial(_mm_rows_kernel, nb=nb, nk=nk, with_next=with_next),
        grid=(nb + 1, nk),
        in_specs=[pl.BlockSpec((tm, tk), lambda i, kk: (jnp.minimum(i, nb - 1), kk)),
                  pl.BlockSpec((tk, D_MODEL), lambda i, kk: (kk, 0)),
                  slice_spec, vec_spec, vec_spec],
        out_specs=out_specs,
        out_shape=out_shape,
        scratch_shapes=[pltpu.VMEM((tm, D_MODEL), F32), pltpu.VMEM((tm, D_MODEL), F32)],
        compiler_params=_params("arbitrary", "arbitrary"),
        name="mm_rows",
    )(a, w, x, g_post.reshape(1, D_MODEL), g_next.reshape(1, D_MODEL))
    return (res[0], res[1]) if with_next else (res[0], None)


OUT_PROJ_PIECES = 8


class _Interleaved:
    def __init__(self, pieces):
        self._pieces = list(pieces)

    def __call__(self):
        if self._pieces:
            self._pieces.pop(0)()

    def flush(self):
        while self._pieces:
            self._pieces.pop(0)()


def _fused_out_proj_kernel(*refs, chunk_body, n_chunk_in):
    chunk_in = refs[:n_chunk_in]
    w_ref, x_ref, gp_ref, gn_ref, xo_ref, hn_ref, mixed_ref, h_ref = refs[n_chunk_in:n_chunk_in + 8]
    chunk_scratch = refs[n_chunk_in + 8:]

    @pl.when(pl.program_id(0) == 0)
    def _():
        mixed_ref[...] = jnp.zeros_like(mixed_ref)

    cols = D_MODEL // OUT_PROJ_PIECES

    def piece(c):
        def run():
            h_ref[:, c * cols:(c + 1) * cols] = jnp.dot(mixed_ref[...], w_ref[:, c * cols:(c + 1) * cols],
                                                        preferred_element_type=F32)
        return run

    chunk_body(*chunk_in, mixed_ref, *chunk_scratch, tick=_Interleaved(piece(c) for c in range(OUT_PROJ_PIECES)))
    x_new = x_ref[...] + _rms(h_ref[...], gp_ref[...])
    xo_ref[...] = x_new
    hn_ref[...] = _rms(x_new, gn_ref[...]).astype(BF16)


def _chunks_with_out_proj(chunk_body, n_steps, rows, chunk_specs, chunk_args, chunk_scratch,
                          w_out, x, g_post, g_next, name):
    k = w_out.shape[0]
    m = x.shape[0]
    trail = pl.BlockSpec((rows, D_MODEL), lambda i: (jnp.maximum(i - 1, 0), 0))
    vec = pl.BlockSpec((1, D_MODEL), lambda i: (0, 0))
    xo, hn = pl.pallas_call(
        functools.partial(_fused_out_proj_kernel, chunk_body=chunk_body, n_chunk_in=len(chunk_specs)),
        grid=(n_steps + 1,),
        in_specs=list(chunk_specs) + [
            pl.BlockSpec((k, D_MODEL), lambda i: (0, 0), pipeline_mode=pl.Buffered(1)), trail, vec, vec],
        out_specs=[trail, trail],
        out_shape=[jax.ShapeDtypeStruct((m, D_MODEL), F32), jax.ShapeDtypeStruct((m, D_MODEL), BF16)],
        scratch_shapes=[pltpu.VMEM((rows, k), BF16), pltpu.VMEM((rows, D_MODEL), F32)] + list(chunk_scratch),
        compiler_params=_params("arbitrary"),
        name=name,
    )(*chunk_args, w_out, x, g_post.reshape(1, D_MODEL), g_next.reshape(1, D_MODEL))
    return xo, hn


def _segment_conv(plane_ref, halo_plane, cw, seg):
    taps = CONV_WIDTH - 1
    sub = lax.broadcasted_iota(jnp.int32, (8, plane_ref.shape[-1]), 0)
    xv = [plane_ref[pl.ds(v, 8, stride=seg), :] for v in range(seg)]
    wrap = [jnp.where(sub == 0,
                      pltpu.roll(halo_plane[taps - d], 1, axis=0),
                      pltpu.roll(xv[seg - d], 1, axis=0)) for d in range(1, taps + 1)]
    for d in range(1, taps + 1):
        halo_plane[taps - d] = xv[seg - d]

    def src(v):
        return xv[v] if v >= 0 else wrap[-v - 1]

    return [cw[0:1] * src(v - 3) + cw[1:2] * src(v - 2) + cw[2:3] * src(v - 1) + cw[3:4] * src(v)
            for v in range(seg)]


GDN_HEAD_GROUP = 16


def _unit_lower_inverses(mats):
    n = mats[0].shape[0]
    row = lax.broadcasted_iota(jnp.int32, (n, n), 0)
    col = lax.broadcasted_iota(jnp.int32, (n, n), 1)
    eye = jnp.where(row == col, 1.0, 0.0).astype(F32)

    def same_block(b):
        shift = int(math.log2(b))
        return (row >> shift) == (col >> shift)

    b = 2
    ds = [eye - jnp.where(same_block(b), a, 0.0) for a in mats]
    while b < n:
        newly = jnp.logical_and(same_block(2 * b), jnp.logical_not(same_block(b)))
        es = [jnp.where(newly, a, 0.0) for a in mats]
        eds = [_mxu(e, d) for e, d in zip(es, ds)]
        ds = [d - _mxu(d, ed) for d, ed in zip(ds, eds)]
        b *= 2
    return ds


def _gdn_kernel(qkv_ref, z_ref, b_ref, a_ref, at_ref, cw_ref, al_ref, dtb_ref, alt_ref, dtbt_ref, ng_ref,
                o_ref, halo_ref, stage_ref, act_ref, s_ref):
    n = GDN_CHUNK
    nh = GDN_HEADS
    seg = n // 8
    planes = stage_ref.shape[0]
    plane_w = stage_ref.shape[-1]

    @pl.when(pl.program_id(0) == 0)
    def _():
        s_ref[...] = jnp.zeros_like(s_ref)
        halo_ref[...] = jnp.zeros_like(halo_ref)

    for c in range(planes):
        stage_ref[c] = qkv_ref[:, c * plane_w:(c + 1) * plane_w]
    for c in range(planes):
        conv = _segment_conv(stage_ref.at[c], halo_ref.at[c], cw_ref[:, c * plane_w:(c + 1) * plane_w], seg)
        for v in range(seg):
            act_ref[c, pl.ds(v, 8, stride=seg), :] = _silu(conv[v])

    beta = _sigmoid(b_ref[...])
    g_col = -jnp.exp(al_ref[...]) * _softplus(a_ref[...] + dtb_ref[...])
    g_row = -jnp.exp(alt_ref[...]) * _softplus(at_ref[...] + dtbt_ref[...])
    gc_col, gc_row = _cumsum_both(g_col, g_row, n)
    incl, strict = _tri_masks(n)
    ng = ng_ref[...]
    g_last = gc_col[n - 1:n, :]
    eg_all = jnp.exp(gc_col)
    beg_all = beta * eg_all
    kdec_all = jnp.exp(g_last - gc_col)
    sdec_all = jnp.exp(g_last)

    def l2norm(t):
        return t * lax.rsqrt(jnp.sum(t * t, axis=-1, keepdims=True) + NORM_EPS)

    for g0 in range(0, GDN_HEADS, GDN_HEAD_GROUP):
        heads = range(g0, g0 + GDN_HEAD_GROUP)
        q = [l2norm(act_ref[h]) * GDN_DK ** -0.5 for h in heads]
        k = [l2norm(act_ref[nh + h]) for h in heads]
        v = [act_ref[2 * nh + h] for h in heads]
        gcc = [gc_col[:, h:h + 1] for h in heads]
        gcr = [gc_row[h:h + 1, :] for h in heads]
        bc = [beta[:, h:h + 1] for h in heads]
        eg = [eg_all[:, h:h + 1] for h in heads]
        dec = [jnp.where(incl, jnp.exp(c - r), 0.0) for c, r in zip(gcc, gcr)]
        prod = [_mxu(jnp.concatenate([ki, qi], axis=0), ki, NT_DIMS) for ki, qi in zip(k, q)]
        a_mat = [jnp.where(strict, b * p[:n] * d, 0.0) for b, p, d in zip(bc, prod, dec)]
        t_inv = _unit_lower_inverses(a_mat)
        rhs = [jnp.concatenate([vi * b, ki * beg_all[:, h:h + 1]], axis=1)
               for vi, ki, b, h in zip(v, k, bc, heads)]
        uw = [r + _mxu(jnp.where(strict, t, 0.0), r) for t, r in zip(t_inv, rhs)]
        state = [s_ref[h] for h in heads]
        qs_ws = [_mxu(jnp.concatenate([qi * e, x[:, GDN_DV:]], axis=0), st)
                 for qi, e, x, st in zip(q, eg, uw, state)]
        v_new = [x[:, :GDN_DV] - y[n:] for x, y in zip(uw, qs_ws)]
        o = [y[:n] + _mxu(p[n:] * d, vn) for y, p, d, vn in zip(qs_ws, prod, dec, v_new)]
        upd = [_mxu(ki * kdec_all[:, h:h + 1], vn, TN_DIMS) for ki, h, vn in zip(k, heads, v_new)]
        for i, h in enumerate(heads):
            s_ref[h] = state[i] * sdec_all[:, h:h + 1] + upd[i]
            zh = z_ref[:, h * GDN_DV:(h + 1) * GDN_DV]
            o_ref[:, h * GDN_DV:(h + 1) * GDN_DV] = (_rms(o[i], ng) * _silu(zh)).astype(BF16)


def _gdn_mixer(hn, w_in, w_out, j, conv_w, a_log, dt_bias, norm_g):
    s = hn.shape[0]
    n = GDN_CHUNK
    nh = GDN_HEADS
    c_qkv = 2 * nh * GDN_DK + nh * GDN_DV
    c_z = nh * GDN_DV
    proj, w_out_b = _mm_cols(hn, jnp.swapaxes(w_in, 1, 2), j, c_qkv + c_z, w_out, j, w_is_transposed=True)
    tail = _mm_tail(hn, w_in[j, :, c_qkv + c_z:])
    b_pre, a_pre = tail[:, :nh], tail[:, nh:]
    a_pre_t = _chunk_rows(a_pre, n)
    full = lambda shape: pl.BlockSpec(shape, lambda i: (0, 0))
    return w_out_b, pl.pallas_call(
        _gdn_kernel,
        grid=(s // n,),
        in_specs=[pl.BlockSpec((n, c_qkv), lambda i: (i, 0)),
                  pl.BlockSpec((n, c_z), lambda i: (i, c_qkv // c_z)),
                  pl.BlockSpec((n, nh), lambda i: (i, 0)),
                  pl.BlockSpec((n, nh), lambda i: (i, 0)),
                  pl.BlockSpec((None, nh, n), lambda i: (i, 0, 0)),
                  full((CONV_WIDTH, c_qkv)),
                  full((1, nh)), full((1, nh)), full((nh, 1)), full((nh, 1)),
                  full((1, GDN_DV))],
        out_specs=pl.BlockSpec((n, c_z), lambda i: (i, 0)),
        out_shape=jax.ShapeDtypeStruct((s, c_z), BF16),
        scratch_shapes=[pltpu.VMEM((c_qkv // GDN_DK, CONV_WIDTH - 1, 8, GDN_DK), F32),
                        pltpu.VMEM((c_qkv // GDN_DK, n, GDN_DK), F32),
                        pltpu.VMEM((c_qkv // GDN_DK, n, GDN_DK), F32),
                        pltpu.VMEM((nh, GDN_DK, GDN_DV), F32)],
        compiler_params=_params("arbitrary"),
        name="gdn_chunks",
    )(proj, proj, b_pre, a_pre, a_pre_t, conv_w,
      a_log.reshape(1, nh), dt_bias.reshape(1, nh), a_log.reshape(nh, 1), dt_bias.reshape(nh, 1),
      norm_g.reshape(1, GDN_DV))


def _softcap(t):
    return GATE_SOFTCAP * jnp.tanh(t / GATE_SOFTCAP)


def _mlstm_kernel(p_ref, i_ref, f_ref, it_ref, ft_ref, bi_ref, bf_ref, bit_ref, bft_ref, ng_ref,
                  o_ref, c_ref, n_ref, m_ref, *, tick):
    n = MLSTM_CHUNK
    nh, dqk, dv = MLSTM_HEADS, MLSTM_DQK, MLSTM_DV

    @pl.when(pl.program_id(0) == 0)
    def _():
        c_ref[...] = jnp.zeros_like(c_ref)
        n_ref[...] = jnp.zeros_like(n_ref)
        m_ref[...] = jnp.zeros_like(m_ref)

    i_col = _softcap(i_ref[...] + bi_ref[...])
    i_row = _softcap(it_ref[...] + bit_ref[...])
    f_col = -_softplus(-_softcap(f_ref[...] + bf_ref[...]))
    f_row = -_softplus(-_softcap(ft_ref[...] + bft_ref[...]))
    b_col, b_row = _cumsum_both(f_col, f_row, n)
    incl, _ = _tri_masks(n)

    heads = range(nh)
    q = [p_ref[:, h * dqk:(h + 1) * dqk] * dqk ** -0.5 for h in heads]
    k = [p_ref[:, nh * dqk + h * dqk:nh * dqk + (h + 1) * dqk] for h in heads]
    v = [p_ref[:, 2 * nh * dqk + h * dv:2 * nh * dqk + (h + 1) * dv] for h in heads]
    bcc = [b_col[:, h:h + 1] for h in heads]
    bcr = [b_row[h:h + 1, :] for h in heads]
    icc = [i_col[:, h:h + 1] for h in heads]
    icr = [i_row[h:h + 1, :] for h in heads]
    m_prev = [m_ref[h:h + 1, 0:1] for h in heads]
    qk = [_mxu(qi, ki, NT_DIMS) for qi, ki in zip(q, k)]
    c_mem = [c_ref[h] for h in heads]
    n_mem = [n_ref[h:h + 1, :] for h in heads]
    qc = [_mxu(qi, c) for qi, c in zip(q, c_mem)]
    d_log = [jnp.where(incl, c - r + i, -jnp.inf) for c, r, i in zip(bcc, bcr, icr)]
    tick()
    inter = [c + m for c, m in zip(bcc, m_prev)]
    m_t = [jnp.maximum(x, jnp.max(d, axis=-1, keepdims=True)) for x, d in zip(inter, d_log)]
    sc = [s * jnp.exp(d - m) for s, d, m in zip(qk, d_log, m_t)]
    tick()
    w_inter = [jnp.exp(x - m) for x, m in zip(inter, m_t)]
    num = [w * x + _mxu(s, vi) for w, x, s, vi in zip(w_inter, qc, sc, v)]
    tick()
    den = [w * jnp.sum(qi * nm, axis=-1, keepdims=True) + jnp.sum(s, axis=-1, keepdims=True)
           for w, qi, nm, s in zip(w_inter, q, n_mem, sc)]
    hid = [x / jnp.maximum(jnp.abs(d), jnp.exp(-m)) for x, d, m in zip(num, den, m_t)]
    tick()
    b_last = [r[:, n - 1:n] for r in bcr]
    end_max = [jnp.max(bl - r + i, axis=-1, keepdims=True) for bl, r, i in zip(b_last, bcr, icr)]
    m_new = [jnp.maximum(bl + m, e) for bl, m, e in zip(b_last, m_prev, end_max)]
    decay = [jnp.exp(bl + m - mn) for bl, m, mn in zip(b_last, m_prev, m_new)]
    k_w = [ki * jnp.exp(bl - c + i - mn) for ki, bl, c, i, mn in zip(k, b_last, bcc, icc, m_new)]
    upd = [_mxu(kw, vi, TN_DIMS) for kw, vi in zip(k_w, v)]
    outs = []
    for h in heads:
        c_ref[h] = decay[h] * c_mem[h] + upd[h]
        n_ref[h:h + 1, :] = decay[h] * n_mem[h] + jnp.sum(k_w[h], axis=0, keepdims=True)
        m_ref[h:h + 1, :] = jnp.broadcast_to(m_new[h], (1, m_ref.shape[1]))
        o_pre = p_ref[:, 2 * nh * dqk + nh * dv + h * dv:2 * nh * dqk + nh * dv + (h + 1) * dv]
        normed = _rms(hid[h], ng_ref[:, h * dv:(h + 1) * dv])
        outs.append((normed * _sigmoid(o_pre)).astype(BF16))
        tick()
    tick.flush()
    for h in heads:
        o_ref[:, h * dv:(h + 1) * dv] = outs[h]


def _mlstm_mixer(hn, w_in, w_out, j, b_i, b_f, norm_g, x, g_post, g_next):
    s = hn.shape[0]
    n = min(MLSTM_CHUNK, s)
    n_steps = s // n
    nh, dqk, dv = MLSTM_HEADS, MLSTM_DQK, MLSTM_DV
    c_main = 2 * nh * dqk + 2 * nh * dv
    proj, w_out_b = _mm_cols(hn, jnp.swapaxes(w_in, 1, 2), j, c_main, w_out, j, w_is_transposed=True)
    tail = _mm_tail(hn, w_in[j, :, c_main:])
    i_pre, f_pre = tail[:, :nh], tail[:, nh:]
    step = lambda i: jnp.minimum(i, n_steps - 1)
    full = lambda shape: pl.BlockSpec(shape, lambda i: (0, 0))
    col = pl.BlockSpec((n, nh), lambda i: (step(i), 0))
    row = pl.BlockSpec((None, nh, n), lambda i: (step(i), 0, 0))
    return _chunks_with_out_proj(
        _mlstm_kernel, n_steps, n,
        [pl.BlockSpec((n, c_main), lambda i: (step(i), 0)), col, col, row, row,
         full((1, nh)), full((1, nh)), full((nh, 1)), full((nh, 1)), full((1, nh * dv))],
        (proj, i_pre, f_pre, _chunk_rows(i_pre, n), _chunk_rows(f_pre, n),
         b_i.reshape(1, nh), b_f.reshape(1, nh), b_i.reshape(nh, 1), b_f.reshape(nh, 1),
         norm_g.reshape(1, nh * dv)),
        [pltpu.VMEM((nh, dqk, dv), F32), pltpu.VMEM((nh, dqk), F32), pltpu.VMEM((nh, 128), F32)],
        w_out_b, x, g_post, g_next, "mlstm_chunks")


def _ret_kernel(p_ref, pos_ref, fr_ref, ng_ref, o_ref, r_ref):
    n = RET_CHUNK
    nh, dk, dv = RET_HEADS, RET_DK, RET_DV
    half = dk // 2

    @pl.when(pl.program_id(0) == 0)
    def _():
        r_ref[...] = jnp.zeros_like(r_ref)

    ang = pos_ref[...].astype(F32) * fr_ref[...]
    cos, sin = jnp.cos(ang), jnp.sin(ang)
    row = lax.broadcasted_iota(jnp.int32, (n, n), 0)
    col = lax.broadcasted_iota(jnp.int32, (n, n), 1)
    lag = (row - col).astype(F32)
    t_col = lax.broadcasted_iota(jnp.int32, (n, 1), 0).astype(F32)

    def rope(t):
        t1, t2 = t[:, :half], t[:, half:]
        return jnp.concatenate([t1 * cos - t2 * sin, t1 * sin + t2 * cos], axis=1)

    heads = range(nh)
    log_gamma = [float(np.log1p(-np.exp2(np.float32(-5.0 - h)))) for h in heads]
    q = [rope(p_ref[:, h * dk:(h + 1) * dk]) for h in heads]
    k = [rope(p_ref[:, nh * dk + h * dk:nh * dk + (h + 1) * dk]) * dk ** -0.5 for h in heads]
    v = [p_ref[:, 2 * nh * dk + h * dv:2 * nh * dk + (h + 1) * dv] for h in heads]
    qk = [_mxu(qi, ki, NT_DIMS) for qi, ki in zip(q, k)]
    mem = [r_ref[h] for h in heads]
    qr = [_mxu(qi, m) for qi, m in zip(q, mem)]
    d_mat = [jnp.where(row >= col, jnp.exp(lag * lg), 0.0) for lg in log_gamma]
    intra = [_mxu(s * d, vi) for s, d, vi in zip(qk, d_mat, v)]
    upd = [_mxu(ki * jnp.exp((n - 1.0 - t_col) * lg), vi, TN_DIMS) for ki, lg, vi in zip(k, log_gamma, v)]
    for h in heads:
        o = intra[h] + qr[h] * jnp.exp((t_col + 1.0) * log_gamma[h])
        r_ref[h] = mem[h] * math.exp(n * log_gamma[h]) + upd[h]
        gate = p_ref[:, 2 * nh * dk + nh * dv + h * dv:2 * nh * dk + nh * dv + (h + 1) * dv]
        o = _rms(o, ng_ref[:, h * dv:(h + 1) * dv])
        o_ref[:, h * dv:(h + 1) * dv] = (o * _silu(gate)).astype(BF16)


def _ret_mixer(hn, positions, w_in, w_out, j, norm_g):
    s = hn.shape[0]
    n = min(RET_CHUNK, s)
    nh, dk, dv = RET_HEADS, RET_DK, RET_DV
    c_all = 2 * nh * dk + 2 * nh * dv
    proj, w_out_b = _mm_cols(hn, w_in, j, c_all, w_out, j)
    freqs = ROPE_BASE ** (-jnp.arange(0, dk, 2, dtype=F32) / dk)
    return w_out_b, pl.pallas_call(
        _ret_kernel,
        grid=(s // n,),
        in_specs=[pl.BlockSpec((n, c_all), lambda i: (i, 0)),
                  pl.BlockSpec((n, 1), lambda i: (i, 0)),
                  pl.BlockSpec((1, dk // 2), lambda i: (0, 0)),
                  pl.BlockSpec((1, nh * dv), lambda i: (0, 0))],
        out_specs=pl.BlockSpec((n, nh * dv), lambda i: (i, 0)),
        out_shape=jax.ShapeDtypeStruct((s, nh * dv), BF16),
        scratch_shapes=[pltpu.VMEM((nh, dk, dv), F32)],
        compiler_params=_params("arbitrary"),
        name="ret_chunks",
    )(proj, positions.reshape(s, 1), freqs.reshape(1, dk // 2), norm_g.reshape(1, nh * dv))


def _gelu_tanh(x):
    return 0.5 * x * (1.0 + jnp.tanh(math.sqrt(2.0 / math.pi) * (x + 0.044715 * (x * x * x))))


def _lru_kernel(p_ref, cw_ref, cb_ref, wr_ref, br_ref, wi_ref, bi_ref, lam_ref, o_ref,
                stage_ref, out_ref, halo_ref, h_ref, *, tick):
    rows = p_ref.shape[0]
    seg = rows // 8
    nb = LRU_BLOCKS
    first = pl.program_id(0) == 0

    @pl.when(first)
    def _():
        h_ref[...] = jnp.zeros_like(h_ref)
        halo_ref[...] = jnp.zeros_like(halo_ref)

    for c in range(2 * nb):
        stage_ref[c] = p_ref[:, c * LRU_BLOCK:(c + 1) * LRU_BLOCK]

    sub = lax.broadcasted_iota(jnp.int32, (8, LRU_BLOCK), 0)
    neg_c_softplus = -LRU_C * _softplus(-lam_ref[...])
    taps = CONV_WIDTH - 1

    for b in range(nb):
        lanes = slice(b * LRU_BLOCK, (b + 1) * LRU_BLOCK)
        xv = [stage_ref[b, pl.ds(v, 8, stride=seg), :] for v in range(seg)]
        wrap = [jnp.where(sub == 0,
                          pltpu.roll(halo_ref[b, taps - d], 1, axis=0),
                          pltpu.roll(xv[seg - d], 1, axis=0)) for d in range(1, taps + 1)]
        for d in range(1, taps + 1):
            halo_ref[b, taps - d] = xv[seg - d]

        def src(v):
            return xv[v] if v >= 0 else wrap[-v - 1]

        cw = cw_ref[:, lanes]
        cb = cb_ref[:, lanes]
        xblk = jnp.concatenate(
            [cw[0:1] * src(v - 3) + cw[1:2] * src(v - 2) + cw[2:3] * src(v - 1) + cw[3:4] * src(v) + cb
             for v in range(seg)], axis=0)
        r = _sigmoid(_mxu(xblk, wr_ref[b]) + br_ref[:, lanes])
        gi = _sigmoid(_mxu(xblk, wi_ref[b]) + bi_ref[:, lanes])
        log_a = r * neg_c_softplus[:, lanes]
        a = jnp.exp(log_a)
        u = jnp.sqrt(1.0 - a * a) * (gi * xblk)

        hs, ps = [u[0:8]], [a[0:8]]
        for v in range(1, seg):
            av = a[v * 8:(v + 1) * 8]
            hs.append(av * hs[-1] + u[v * 8:(v + 1) * 8])
            ps.append(av * ps[-1])
        e, pc = hs[-1], ps[-1]
        d = 1
        while d < 8:
            keep = sub >= d
            e = e + pc * jnp.where(keep, pltpu.roll(e, d, axis=0), 0.0)
            pc = pc * jnp.where(keep, pltpu.roll(pc, d, axis=0), 1.0)
            d *= 2
        h0 = h_ref[b, 0:1, :]
        end_state = e + pc * h0
        carry = jnp.where(sub == 0, h0, pltpu.roll(end_state, 1, axis=0))
        h_ref[b, 0:1, :] = end_state[7:8]
        for v in range(seg):
            gate = stage_ref[nb + b, pl.ds(v, 8, stride=seg), :]
            out_ref[b, pl.ds(v, 8, stride=seg), :] = (hs[v] + ps[v] * carry) * _gelu_tanh(gate)
        if b % 2:
            tick()

    tick.flush()
    for b in range(nb):
        o_ref[:, b * LRU_BLOCK:(b + 1) * LRU_BLOCK] = out_ref[b].astype(BF16)


def _lru_mixer(hn, w_in, w_out, j, conv_w, conv_b, w_r, b_r, w_i, b_i, lam, x, g_post, g_next):
    s = hn.shape[0]
    rows = min(LRU_ROWS, s)
    n_steps = s // rows
    w = LRU_WIDTH
    proj, w_out_b = _mm_cols(hn, w_in, j, 2 * w, w_out, j)
    vec = pl.BlockSpec((1, w), lambda i: (0, 0))
    blk = pl.BlockSpec((LRU_BLOCKS, LRU_BLOCK, LRU_BLOCK), lambda i: (0, 0, 0))
    return _chunks_with_out_proj(
        _lru_kernel, n_steps, rows,
        [pl.BlockSpec((rows, 2 * w), lambda i: (jnp.minimum(i, n_steps - 1), 0)),
         pl.BlockSpec((CONV_WIDTH, w), lambda i: (0, 0)), vec, blk, vec, blk, vec, vec],
        (proj, conv_w, conv_b.reshape(1, w), w_r, b_r.reshape(1, w), w_i, b_i.reshape(1, w), lam.reshape(1, w)),
        [pltpu.VMEM((2 * LRU_BLOCKS, rows, LRU_BLOCK), F32),
         pltpu.VMEM((LRU_BLOCKS, rows, LRU_BLOCK), F32),
         pltpu.VMEM((LRU_BLOCKS, CONV_WIDTH - 1, 8, LRU_BLOCK), F32),
         pltpu.VMEM((LRU_BLOCKS, 8, LRU_BLOCK), F32)],
        w_out_b, x, g_post, g_next, "lru_scan")


@jax.jit
def kernel(x, positions, norm_g, mlp_w_up, mlp_w_down, gdn_w_in, gdn_conv_w, gdn_a_log, gdn_dt_bias, gdn_norm_g, gdn_w_out, mlstm_w_in, mlstm_b_i, mlstm_b_f, mlstm_norm_g, mlstm_w_out, ret_w_in, ret_norm_g, ret_w_out, lru_w_in, lru_conv_w, lru_conv_b, lru_w_r, lru_b_r, lru_w_i, lru_b_i, lru_lambda, lru_w_out):
    batch, seq, _ = x.shape
    assert batch == 1
    depth = norm_g.shape[0]
    xs = x.reshape(seq, D_MODEL)
    hn = _prenorm(xs, norm_g[0, 0])
    for layer in range(depth):
        kind, j = layer % 4, layer // 4
        if kind == 0:
            w_out, mixed = _gdn_mixer(hn, gdn_w_in, gdn_w_out, j, gdn_conv_w[j], gdn_a_log[j], gdn_dt_bias[j],
                                      gdn_norm_g[j])
            xs, hn = _mm_rows(mixed, w_out, xs, norm_g[layer, 1], norm_g[layer, 2])
        elif kind == 2:
            w_out, mixed = _ret_mixer(hn, positions, ret_w_in, ret_w_out, j, ret_norm_g[j])
            xs, hn = _mm_rows(mixed, w_out, xs, norm_g[layer, 1], norm_g[layer, 2])
        elif kind == 1:
            xs, hn = _mlstm_mixer(hn, mlstm_w_in, mlstm_w_out, j, mlstm_b_i[j], mlstm_b_f[j], mlstm_norm_g[j],
                                  xs, norm_g[layer, 1], norm_g[layer, 2])
        else:
            xs, hn = _lru_mixer(hn, lru_w_in, lru_w_out, j, lru_conv_w[j], lru_conv_b[j], lru_w_r[j],
                                lru_b_r[j], lru_w_i[j], lru_b_i[j], lru_lambda[j],
                                xs, norm_g[layer, 1], norm_g[layer, 2])
        hmid, w_down = _mm_cols(hn, mlp_w_up, layer, D_FF, mlp_w_down, layer, act="relu2", out_dtype=BF16)
        g_next = norm_g[layer + 1, 0] if layer + 1 < depth else None
        xs, hn = _mm_rows(hmid, w_down, xs, norm_g[layer, 3], g_next)
    return xs.reshape(batch, seq, D_MODEL)
```

```python
import functools
import math

import numpy as np
import jax
import jax.numpy as jnp
from jax import lax
from jax.experimental import pallas as pl
from jax.experimental.pallas import tpu as pltpu

F32 = jnp.float32
BF16 = jnp.bfloat16

D_MODEL = 2048
D_FF = 4 * D_MODEL
NORM_EPS = 1e-6
CONV_WIDTH = 4

GDN_HEADS, GDN_DK, GDN_DV = 16, 128, 128
GDN_CHUNK = 128
MLSTM_HEADS, MLSTM_DQK, MLSTM_DV = 8, 128, 256
MLSTM_CHUNK = 256
GATE_SOFTCAP = 15.0
RET_HEADS, RET_DK, RET_DV = 8, 256, 512
RET_CHUNK = 256
ROPE_BASE = 10000.0
LRU_WIDTH, LRU_BLOCKS, LRU_BLOCK = 2048, 16, 128
LRU_ROWS = 128
LRU_C = 8.0

VMEM_LIMIT_BYTES = 56 * 1024 * 1024
W_RING = 3

NT_DIMS = (((1,), (1,)), ((), ()))
TN_DIMS = (((0,), (0,)), ((), ()))


def _params(*semantics):
    return pltpu.CompilerParams(dimension_semantics=semantics,
                                vmem_limit_bytes=VMEM_LIMIT_BYTES)


def _rms(x, g):
    y = x * lax.rsqrt(jnp.mean(x * x, axis=-1, keepdims=True) + NORM_EPS)
    return y * g


def _sigmoid(x):
    return 1.0 / (1.0 + jnp.exp(-x))


def _silu(x):
    return x * _sigmoid(x)


def _softplus(x):
    return jnp.maximum(x, 0.0) + jnp.log1p(jnp.exp(-jnp.abs(x)))


def _mxu(a, b, dims=None):
    a = a.astype(BF16)
    b = b.astype(BF16)
    if dims is None:
        return jnp.dot(a, b, preferred_element_type=F32)
    return lax.dot_general(a, b, dims, preferred_element_type=F32)


def _split3(a):
    hi = a.astype(BF16).astype(F32)
    rest = a - hi
    mid = rest.astype(BF16).astype(F32)
    return hi, mid, (rest - mid).astype(BF16).astype(F32)


def _tri_masks(n):
    row = lax.broadcasted_iota(jnp.int32, (n, n), 0)
    col = lax.broadcasted_iota(jnp.int32, (n, n), 1)
    return row >= col, row > col


def _cumsum_both(col_vals, row_vals, n):
    h = col_vals.shape[1]
    row = lax.broadcasted_iota(jnp.int32, (n, n), 0)
    col = lax.broadcasted_iota(jnp.int32, (n, n), 1)
    lower = jnp.where(row >= col, 1.0, 0.0).astype(BF16)
    upper = jnp.where(row <= col, 1.0, 0.0).astype(BF16)
    cs = jnp.dot(lower, jnp.concatenate(_split3(col_vals), axis=1).astype(BF16), preferred_element_type=F32)
    rs = jnp.dot(jnp.concatenate(_split3(row_vals), axis=0).astype(BF16), upper, preferred_element_type=F32)
    return (cs[:, 2 * h:] + cs[:, h:2 * h]) + cs[:, :h], (rs[2 * h:] + rs[h:2 * h]) + rs[:h]


def _prenorm_kernel(x_ref, g_ref, o_ref):
    o_ref[...] = _rms(x_ref[...], g_ref[...]).astype(o_ref.dtype)


def _prenorm(x, g):
    s = x.shape[0]
    tm = 512
    return pl.pallas_call(
        _prenorm_kernel,
        grid=(s // tm,),
        in_specs=[pl.BlockSpec((tm, D_MODEL), lambda i: (i, 0)),
                  pl.BlockSpec((1, D_MODEL), lambda i: (0, 0))],
        out_specs=pl.BlockSpec((tm, D_MODEL), lambda i: (i, 0)),
        out_shape=jax.ShapeDtypeStruct((s, D_MODEL), BF16),
        compiler_params=_params("parallel"),
        name="prenorm",
    )(x, g.reshape(1, D_MODEL))


def _mm_cols_kernel(x_ref, w_ref, s_ref, o_ref, so_ref, wb_ref, *, act, w_is_transposed):
    @pl.when(pl.program_id(1) == 0)
    def _():
        wb_ref[...] = w_ref[...].astype(BF16)

    if w_is_transposed:
        acc = lax.dot_general(x_ref[...], wb_ref[...], NT_DIMS, preferred_element_type=F32)
    else:
        acc = jnp.dot(x_ref[...], wb_ref[...], preferred_element_type=F32)
    if act == "relu2":
        acc = jnp.square(jnp.maximum(acc, 0.0))
    o_ref[...] = acc.astype(o_ref.dtype)
    so_ref[...] = s_ref[...].astype(BF16)


def _side_rows(n_rows, n_steps):
    rows = 16
    while rows * n_steps < n_rows or n_rows % rows:
        rows += 16
    return rows


def _mm_cols(xb, w, layer, n_cols, side, side_layer, act=None, out_dtype=F32, w_is_transposed=False,
             tm=1024, tn=1024):
    m, k = xb.shape
    tm = min(tm, m)
    ni = m // tm
    nj = n_cols // tn
    _, sr, sc = side.shape
    rb = _side_rows(sr, ni * nj)
    last = sr // rb - 1
    side_idx = lambda j, i: jnp.minimum(j * ni + i, last)
    if w_is_transposed:
        w_spec = pl.BlockSpec((None, tn, k), lambda j, i: (layer, j, 0))
        wb_shape = (tn, k)
    else:
        w_spec = pl.BlockSpec((None, k, tn), lambda j, i: (layer, 0, j))
        wb_shape = (k, tn)
    return pl.pallas_call(
        functools.partial(_mm_cols_kernel, act=act, w_is_transposed=w_is_transposed),
        grid=(nj, ni),
        in_specs=[pl.BlockSpec((tm, k), lambda j, i: (i, 0)),
                  w_spec,
                  pl.BlockSpec((None, rb, sc), lambda j, i: (side_layer, side_idx(j, i), 0))],
        out_specs=[pl.BlockSpec((tm, tn), lambda j, i: (i, j)),
                   pl.BlockSpec((rb, sc), lambda j, i: (side_idx(j, i), 0))],
        out_shape=[jax.ShapeDtypeStruct((m, n_cols), out_dtype),
                   jax.ShapeDtypeStruct((sr, sc), BF16)],
        scratch_shapes=[pltpu.VMEM(wb_shape, BF16)],
        compiler_params=_params("arbitrary", "arbitrary"),
        name="mm_cols",
    )(xb, w, side)


def _tail_kernel(x_ref, w_ref, o_ref):
    o_ref[...] = jnp.dot(x_ref[...], w_ref[...].astype(BF16), preferred_element_type=F32)


def _mm_tail(xb, w_tail):
    m, k = xb.shape
    nt = w_tail.shape[1]
    tm = min(512, m)
    return pl.pallas_call(
        _tail_kernel,
        grid=(m // tm,),
        in_specs=[pl.BlockSpec((tm, k), lambda i: (i, 0)),
                  pl.BlockSpec((k, nt), lambda i: (0, 0))],
        out_specs=pl.BlockSpec((tm, nt), lambda i: (i, 0)),
        out_shape=jax.ShapeDtypeStruct((m, nt), F32),
        compiler_params=_params("parallel"),
        name="mm_tail",
    )(xb, w_tail)


def _chunk_rows(col_vals, n):
    s, nh = col_vals.shape
    return jnp.swapaxes(col_vals.reshape(s // n, n, nh), 1, 2)


def _mm_rows_kernel(a_ref, w_ref, x_ref, gp_ref, gn_ref, *refs, nb, nk, with_next):
    xo_ref = refs[0]
    hn_ref = refs[1] if with_next else None
    n_out = 2 if with_next else 1
    accs = refs[n_out:n_out + 2]
    i = pl.program_id(0)
    k = pl.program_id(1)
    rows = x_ref.shape[0]

    if nk > 1:
        wbuf_ref, sem_ref = refs[n_out + 2:]
        tk = wbuf_ref.shape[1]
        step = i * nk + k
        n_mm = nb * nk

        def tile_copy(s):
            slot = s % W_RING
            src = w_ref.at[pl.ds(pl.multiple_of((s % nk) * tk, tk), tk), :]
            return pltpu.make_async_copy(src, wbuf_ref.at[slot], sem_ref.at[slot])

        @pl.when(step == 0)
        def _():
            for s in range(W_RING - 1):
                tile_copy(s).start()

        @pl.when(step + W_RING - 1 < n_mm)
        def _():
            tile_copy(step + W_RING - 1).start()

        @pl.when(step < n_mm)
        def _():
            tile_copy(step).wait()

        def w_tile():
            return wbuf_ref[step % W_RING]
    else:
        def w_tile():
            return w_ref[...]

    def matmul(acc_ref):
        acc_ref[...] += jnp.dot(a_ref[...], w_tile(), preferred_element_type=F32)

    def epilogue(acc_ref):
        rows_k = pl.ds(pl.multiple_of(k * rows, rows), rows)
        acc = acc_ref[rows_k, :]
        acc_ref[rows_k, :] = jnp.zeros_like(acc)
        x_new = x_ref[...] + _rms(acc, gp_ref[...])
        xo_ref[...] = x_new
        if with_next:
            hn_ref[...] = _rms(x_new, gn_ref[...]).astype(BF16)

    @pl.when(jnp.logical_and(i == 0, k == 0))
    def _():
        for acc_ref in accs:
            acc_ref[...] = jnp.zeros_like(acc_ref)

    @pl.when(i == 0)
    def _():
        matmul(accs[0])

    for parity in (0, 1):
        @pl.when(jnp.logical_and(jnp.logical_and(i > 0, i < nb), i % 2 == parity))
        def _():
            epilogue(accs[1 - parity])
            matmul(accs[parity])

    @pl.when(i == nb)
    def _():
        epilogue(accs[(nb - 1) % 2])


def _mm_rows_tiles(m, k):
    if k <= D_MODEL:
        return min(512, m), k
    return min(1024, m), 1024


def _mm_rows(a, w, x, g_post, g_next):
    m, k = a.shape
    tm, tk = _mm_rows_tiles(m, k)
    nb, nk = m // tm, k // tk
    rows = tm // nk
    with_next = g_next is not None
    if g_next is None:
        g_next = g_post
    slice_spec = pl.BlockSpec((rows, D_MODEL), lambda i, kk: (jnp.maximum((i - 1) * nk + kk, 0), 0))
    vec_spec = pl.BlockSpec((1, D_MODEL), lambda i, kk: (0, 0))
    scratch = [pltpu.VMEM((tm, D_MODEL), F32), pltpu.VMEM((tm, D_MODEL), F32)]
    if nk > 1:
        assert nb * nk >= W_RING - 1
        w_spec = pl.BlockSpec(memory_space=pl.ANY)
        scratch += [pltpu.VMEM((W_RING, tk, D_MODEL), BF16), pltpu.SemaphoreType.DMA((W_RING,))]
    else:
        w_spec = pl.BlockSpec((tk, D_MODEL), lambda i, kk: (kk, 0))
    out_specs = [slice_spec]
    out_shape = [jax.ShapeDtypeStruct((m, D_MODEL), F32)]
    if with_next:
        out_specs.append(slice_spec)
        out_shape.append(jax.ShapeDtypeStruct((m, D_MODEL), BF16))
    res = pl.pallas_call(
        functools.partial(_mm_rows_kernel, nb=nb, nk=nk, with_next=with_next),
        grid=(nb + 1, nk),
        in_specs=[pl.BlockSpec((tm, tk), lambda i, kk: (jnp.minimum(i, nb - 1), kk)),
                  w_spec, slice_spec, vec_spec, vec_spec],
        out_specs=out_specs,
        out_shape=out_shape,
        scratch_shapes=scratch,
        compiler_params=_params("arbitrary", "arbitrary"),
        name="mm_rows",
    )(a, w, x, g_post.reshape(1, D_MODEL), g_next.reshape(1, D_MODEL))
    return (res[0], res[1]) if with_next else (res[0], None)


OUT_PROJ_PIECES = 8


class _Interleaved:
    def __init__(self, pieces):
        self._pieces = list(pieces)

    def __call__(self):
        if self._pieces:
            self._pieces.pop(0)()

    def flush(self):
        while self._pieces:
            self._pieces.pop(0)()


def _fused_out_proj_kernel(*refs, chunk_body, n_chunk_in):
    chunk_in = refs[:n_chunk_in]
    w_ref, x_ref, gp_ref, gn_ref, xo_ref, hn_ref, mixed_ref, h_ref = refs[n_chunk_in:n_chunk_in + 8]
    chunk_scratch = refs[n_chunk_in + 8:]

    @pl.when(pl.program_id(0) == 0)
    def _():
        mixed_ref[...] = jnp.zeros_like(mixed_ref)

    cols = D_MODEL // OUT_PROJ_PIECES

    def piece(c):
        def run():
            h_ref[:, c * cols:(c + 1) * cols] = jnp.dot(mixed_ref[...], w_ref[:, c * cols:(c + 1) * cols],
                                                        preferred_element_type=F32)
        return run

    chunk_body(*chunk_in, mixed_ref, *chunk_scratch, tick=_Interleaved(piece(c) for c in range(OUT_PROJ_PIECES)))
    x_new = x_ref[...] + _rms(h_ref[...], gp_ref[...])
    xo_ref[...] = x_new
    hn_ref[...] = _rms(x_new, gn_ref[...]).astype(BF16)


def _chunks_with_out_proj(chunk_body, n_steps, rows, chunk_specs, chunk_args, chunk_scratch,
                          w_out, x, g_post, g_next, name):
    k = w_out.shape[0]
    m = x.shape[0]
    trail = pl.BlockSpec((rows, D_MODEL), lambda i: (jnp.maximum(i - 1, 0), 0))
    vec = pl.BlockSpec((1, D_MODEL), lambda i: (0, 0))
    xo, hn = pl.pallas_call(
        functools.partial(_fused_out_proj_kernel, chunk_body=chunk_body, n_chunk_in=len(chunk_specs)),
        grid=(n_steps + 1,),
        in_specs=list(chunk_specs) + [
            pl.BlockSpec((k, D_MODEL), lambda i: (0, 0), pipeline_mode=pl.Buffered(1)), trail, vec, vec],
        out_specs=[trail, trail],
        out_shape=[jax.ShapeDtypeStruct((m, D_MODEL), F32), jax.ShapeDtypeStruct((m, D_MODEL), BF16)],
        scratch_shapes=[pltpu.VMEM((rows, k), BF16), pltpu.VMEM((rows, D_MODEL), F32)] + list(chunk_scratch),
        compiler_params=_params("arbitrary"),
        name=name,
    )(*chunk_args, w_out, x, g_post.reshape(1, D_MODEL), g_next.reshape(1, D_MODEL))
    return xo, hn


def _segment_conv(plane_ref, halo_plane, cw, seg):
    taps = CONV_WIDTH - 1
    sub = lax.broadcasted_iota(jnp.int32, (8, plane_ref.shape[-1]), 0)
    xv = [plane_ref[pl.ds(v, 8, stride=seg), :] for v in range(seg)]
    wrap = [jnp.where(sub == 0,
                      pltpu.roll(halo_plane[taps - d], 1, axis=0),
                      pltpu.roll(xv[seg - d], 1, axis=0)) for d in range(1, taps + 1)]
    for d in range(1, taps + 1):
        halo_plane[taps - d] = xv[seg - d]

    def src(v):
        return xv[v] if v >= 0 else wrap[-v - 1]

    return [cw[0:1] * src(v - 3) + cw[1:2] * src(v - 2) + cw[2:3] * src(v - 1) + cw[3:4] * src(v)
            for v in range(seg)]


GDN_HEAD_GROUP = 16


def _unit_lower_inverses(mats):
    n = mats[0].shape[0]
    row = lax.broadcasted_iota(jnp.int32, (n, n), 0)
    col = lax.broadcasted_iota(jnp.int32, (n, n), 1)
    eye = jnp.where(row == col, 1.0, 0.0).astype(F32)

    def same_block(b):
        shift = int(math.log2(b))
        return (row >> shift) == (col >> shift)

    b = 2
    ds = [eye - jnp.where(same_block(b), a, 0.0) for a in mats]
    while b < n:
        newly = jnp.logical_and(same_block(2 * b), jnp.logical_not(same_block(b)))
        es = [jnp.where(newly, a, 0.0) for a in mats]
        eds = [_mxu(e, d) for e, d in zip(es, ds)]
        ds = [d - _mxu(d, ed) for d, ed in zip(ds, eds)]
        b *= 2
    return ds


def _gdn_kernel(qkv_ref, z_ref, b_ref, a_ref, at_ref, cw_ref, al_ref, dtb_ref, alt_ref, dtbt_ref, ng_ref,
                o_ref, halo_ref, stage_ref, act_ref, s_ref):
    n = GDN_CHUNK
    nh = GDN_HEADS
    seg = n // 8
    planes = stage_ref.shape[0]
    plane_w = stage_ref.shape[-1]

    @pl.when(pl.program_id(0) == 0)
    def _():
        s_ref[...] = jnp.zeros_like(s_ref)
        halo_ref[...] = jnp.zeros_like(halo_ref)

    for c in range(planes):
        stage_ref[c] = qkv_ref[:, c * plane_w:(c + 1) * plane_w]
    for c in range(planes):
        conv = _segment_conv(stage_ref.at[c], halo_ref.at[c], cw_ref[:, c * plane_w:(c + 1) * plane_w], seg)
        for v in range(seg):
            act_ref[c, pl.ds(v, 8, stride=seg), :] = _silu(conv[v])

    beta = _sigmoid(b_ref[...])
    g_col = -jnp.exp(al_ref[...]) * _softplus(a_ref[...] + dtb_ref[...])
    g_row = -jnp.exp(alt_ref[...]) * _softplus(at_ref[...] + dtbt_ref[...])
    gc_col, gc_row = _cumsum_both(g_col, g_row, n)
    incl, strict = _tri_masks(n)
    ng = ng_ref[...]
    g_last = gc_col[n - 1:n, :]
    eg_all = jnp.exp(gc_col)
    beg_all = beta * eg_all
    kdec_all = jnp.exp(g_last - gc_col)
    sdec_all = jnp.exp(g_last)

    def l2norm(t):
        return t * lax.rsqrt(jnp.sum(t * t, axis=-1, keepdims=True) + NORM_EPS)

    for g0 in range(0, GDN_HEADS, GDN_HEAD_GROUP):
        heads = range(g0, g0 + GDN_HEAD_GROUP)
        q = [l2norm(act_ref[h]) * GDN_DK ** -0.5 for h in heads]
        k = [l2norm(act_ref[nh + h]) for h in heads]
        v = [act_ref[2 * nh + h] for h in heads]
        gcc = [gc_col[:, h:h + 1] for h in heads]
        gcr = [gc_row[h:h + 1, :] for h in heads]
        bc = [beta[:, h:h + 1] for h in heads]
        eg = [eg_all[:, h:h + 1] for h in heads]
        dec = [jnp.where(incl, jnp.exp(c - r), 0.0) for c, r in zip(gcc, gcr)]
        prod = [_mxu(jnp.concatenate([ki, qi], axis=0), ki, NT_DIMS) for ki, qi in zip(k, q)]
        a_mat = [jnp.where(strict, b * p[:n] * d, 0.0) for b, p, d in zip(bc, prod, dec)]
        t_inv = _unit_lower_inverses(a_mat)
        rhs = [jnp.concatenate([vi * b, ki * beg_all[:, h:h + 1]], axis=1)
               for vi, ki, b, h in zip(v, k, bc, heads)]
        uw = [r + _mxu(jnp.where(strict, t, 0.0), r) for t, r in zip(t_inv, rhs)]
        state = [s_ref[h] for h in heads]
        qs_ws = [_mxu(jnp.concatenate([qi * e, x[:, GDN_DV:]], axis=0), st)
                 for qi, e, x, st in zip(q, eg, uw, state)]
        v_new = [x[:, :GDN_DV] - y[n:] for x, y in zip(uw, qs_ws)]
        o = [y[:n] + _mxu(p[n:] * d, vn) for y, p, d, vn in zip(qs_ws, prod, dec, v_new)]
        upd = [_mxu(ki * kdec_all[:, h:h + 1], vn, TN_DIMS) for ki, h, vn in zip(k, heads, v_new)]
        for i, h in enumerate(heads):
            s_ref[h] = state[i] * sdec_all[:, h:h + 1] + upd[i]
            zh = z_ref[:, h * GDN_DV:(h + 1) * GDN_DV]
            o_ref[:, h * GDN_DV:(h + 1) * GDN_DV] = (_rms(o[i], ng) * _silu(zh)).astype(BF16)


def _gdn_mixer(hn, w_in, w_out, j, conv_w, a_log, dt_bias, norm_g):
    s = hn.shape[0]
    n = GDN_CHUNK
    nh = GDN_HEADS
    c_qkv = 2 * nh * GDN_DK + nh * GDN_DV
    c_z = nh * GDN_DV
    proj, w_out_b = _mm_cols(hn, jnp.swapaxes(w_in, 1, 2), j, c_qkv + c_z, w_out, j, w_is_transposed=True)
    tail = _mm_tail(hn, w_in[j, :, c_qkv + c_z:])
    b_pre, a_pre = tail[:, :nh], tail[:, nh:]
    a_pre_t = _chunk_rows(a_pre, n)
    full = lambda shape: pl.BlockSpec(shape, lambda i: (0, 0))
    return w_out_b, pl.pallas_call(
        _gdn_kernel,
        grid=(s // n,),
        in_specs=[pl.BlockSpec((n, c_qkv), lambda i: (i, 0)),
                  pl.BlockSpec((n, c_z), lambda i: (i, c_qkv // c_z)),
                  pl.BlockSpec((n, nh), lambda i: (i, 0)),
                  pl.BlockSpec((n, nh), lambda i: (i, 0)),
                  pl.BlockSpec((None, nh, n), lambda i: (i, 0, 0)),
                  full((CONV_WIDTH, c_qkv)),
                  full((1, nh)), full((1, nh)), full((nh, 1)), full((nh, 1)),
                  full((1, GDN_DV))],
        out_specs=pl.BlockSpec((n, c_z), lambda i: (i, 0)),
        out_shape=jax.ShapeDtypeStruct((s, c_z), BF16),
        scratch_shapes=[pltpu.VMEM((c_qkv // GDN_DK, CONV_WIDTH - 1, 8, GDN_DK), F32),
                        pltpu.VMEM((c_qkv // GDN_DK, n, GDN_DK), F32),
                        pltpu.VMEM((c_qkv // GDN_DK, n, GDN_DK), F32),
                        pltpu.VMEM((nh, GDN_DK, GDN_DV), F32)],
        compiler_params=_params("arbitrary"),
        name="gdn_chunks",
    )(proj, proj, b_pre, a_pre, a_pre_t, conv_w,
      a_log.reshape(1, nh), dt_bias.reshape(1, nh), a_log.reshape(nh, 1), dt_bias.reshape(nh, 1),
      norm_g.reshape(1, GDN_DV))


def _softcap(t):
    return GATE_SOFTCAP * jnp.tanh(t / GATE_SOFTCAP)


def _mlstm_kernel(p_ref, i_ref, f_ref, it_ref, ft_ref, bi_ref, bf_ref, bit_ref, bft_ref, ng_ref,
                  o_ref, c_ref, n_ref, m_ref, *, tick):
    n = MLSTM_CHUNK
    nh, dqk, dv = MLSTM_HEADS, MLSTM_DQK, MLSTM_DV

    @pl.when(pl.program_id(0) == 0)
    def _():
        c_ref[...] = jnp.zeros_like(c_ref)
        n_ref[...] = jnp.zeros_like(n_ref)
        m_ref[...] = jnp.zeros_like(m_ref)

    i_col = _softcap(i_ref[...] + bi_ref[...])
    i_row = _softcap(it_ref[...] + bit_ref[...])
    f_col = -_softplus(-_softcap(f_ref[...] + bf_ref[...]))
    f_row = -_softplus(-_softcap(ft_ref[...] + bft_ref[...]))
    b_col, b_row = _cumsum_both(f_col, f_row, n)
    incl, _ = _tri_masks(n)

    heads = range(nh)
    q = [p_ref[:, h * dqk:(h + 1) * dqk] * dqk ** -0.5 for h in heads]
    k = [p_ref[:, nh * dqk + h * dqk:nh * dqk + (h + 1) * dqk] for h in heads]
    v = [p_ref[:, 2 * nh * dqk + h * dv:2 * nh * dqk + (h + 1) * dv] for h in heads]
    bcc = [b_col[:, h:h + 1] for h in heads]
    bcr = [b_row[h:h + 1, :] for h in heads]
    icc = [i_col[:, h:h + 1] for h in heads]
    icr = [i_row[h:h + 1, :] for h in heads]
    m_prev = [m_ref[h:h + 1, 0:1] for h in heads]
    qk = [_mxu(qi, ki, NT_DIMS) for qi, ki in zip(q, k)]
    c_mem = [c_ref[h] for h in heads]
    n_mem = [n_ref[h:h + 1, :] for h in heads]
    qc = [_mxu(qi, c) for qi, c in zip(q, c_mem)]
    d_log = [jnp.where(incl, c - r + i, -jnp.inf) for c, r, i in zip(bcc, bcr, icr)]
    tick()
    inter = [c + m for c, m in zip(bcc, m_prev)]
    m_t = [jnp.maximum(x, jnp.max(d, axis=-1, keepdims=True)) for x, d in zip(inter, d_log)]
    sc = [s * jnp.exp(d - m) for s, d, m in zip(qk, d_log, m_t)]
    tick()
    w_inter = [jnp.exp(x - m) for x, m in zip(inter, m_t)]
    num = [w * x + _mxu(s, vi) for w, x, s, vi in zip(w_inter, qc, sc, v)]
    tick()
    den = [w * jnp.sum(qi * nm, axis=-1, keepdims=True) + jnp.sum(s, axis=-1, keepdims=True)
           for w, qi, nm, s in zip(w_inter, q, n_mem, sc)]
    hid = [x / jnp.maximum(jnp.abs(d), jnp.exp(-m)) for x, d, m in zip(num, den, m_t)]
    tick()
    b_last = [r[:, n - 1:n] for r in bcr]
    end_max = [jnp.max(bl - r + i, axis=-1, keepdims=True) for bl, r, i in zip(b_last, bcr, icr)]
    m_new = [jnp.maximum(bl + m, e) for bl, m, e in zip(b_last, m_prev, end_max)]
    decay = [jnp.exp(bl + m - mn) for bl, m, mn in zip(b_last, m_prev, m_new)]
    k_w = [ki * jnp.exp(bl - c + i - mn) for ki, bl, c, i, mn in zip(k, b_last, bcc, icc, m_new)]
    upd = [_mxu(kw, vi, TN_DIMS) for kw, vi in zip(k_w, v)]
    outs = []
    for h in heads:
        c_ref[h] = decay[h] * c_mem[h] + upd[h]
        n_ref[h:h + 1, :] = decay[h] * n_mem[h] + jnp.sum(k_w[h], axis=0, keepdims=True)
        m_ref[h:h + 1, :] = jnp.broadcast_to(m_new[h], (1, m_ref.shape[1]))
        o_pre = p_ref[:, 2 * nh * dqk + nh * dv + h * dv:2 * nh * dqk + nh * dv + (h + 1) * dv]
        normed = _rms(hid[h], ng_ref[:, h * dv:(h + 1) * dv])
        outs.append((normed * _sigmoid(o_pre)).astype(BF16))
        tick()
    tick.flush()
    for h in heads:
        o_ref[:, h * dv:(h + 1) * dv] = outs[h]


def _mlstm_mixer(hn, w_in, w_out, j, b_i, b_f, norm_g, x, g_post, g_next):
    s = hn.shape[0]
    n = min(MLSTM_CHUNK, s)
    n_steps = s // n
    nh, dqk, dv = MLSTM_HEADS, MLSTM_DQK, MLSTM_DV
    c_main = 2 * nh * dqk + 2 * nh * dv
    proj, w_out_b = _mm_cols(hn, jnp.swapaxes(w_in, 1, 2), j, c_main, w_out, j, w_is_transposed=True)
    tail = _mm_tail(hn, w_in[j, :, c_main:])
    i_pre, f_pre = tail[:, :nh], tail[:, nh:]
    step = lambda i: jnp.minimum(i, n_steps - 1)
    full = lambda shape: pl.BlockSpec(shape, lambda i: (0, 0))
    col = pl.BlockSpec((n, nh), lambda i: (step(i), 0))
    row = pl.BlockSpec((None, nh, n), lambda i: (step(i), 0, 0))
    return _chunks_with_out_proj(
        _mlstm_kernel, n_steps, n,
        [pl.BlockSpec((n, c_main), lambda i: (step(i), 0)), col, col, row, row,
         full((1, nh)), full((1, nh)), full((nh, 1)), full((nh, 1)), full((1, nh * dv))],
        (proj, i_pre, f_pre, _chunk_rows(i_pre, n), _chunk_rows(f_pre, n),
         b_i.reshape(1, nh), b_f.reshape(1, nh), b_i.reshape(nh, 1), b_f.reshape(nh, 1),
         norm_g.reshape(1, nh * dv)),
        [pltpu.VMEM((nh, dqk, dv), F32), pltpu.VMEM((nh, dqk), F32), pltpu.VMEM((nh, 128), F32)],
        w_out_b, x, g_post, g_next, "mlstm_chunks")


def _ret_kernel(p_ref, pos_ref, fr_ref, ng_ref, o_ref, r_ref):
    n = RET_CHUNK
    nh, dk, dv = RET_HEADS, RET_DK, RET_DV
    half = dk // 2

    @pl.when(pl.program_id(0) == 0)
    def _():
        r_ref[...] = jnp.zeros_like(r_ref)

    ang = pos_ref[...].astype(F32) * fr_ref[...]
    cos, sin = jnp.cos(ang), jnp.sin(ang)
    row = lax.broadcasted_iota(jnp.int32, (n, n), 0)
    col = lax.broadcasted_iota(jnp.int32, (n, n), 1)
    lag = (row - col).astype(F32)
    t_col = lax.broadcasted_iota(jnp.int32, (n, 1), 0).astype(F32)

    def rope(t):
        t1, t2 = t[:, :half], t[:, half:]
        return jnp.concatenate([t1 * cos - t2 * sin, t1 * sin + t2 * cos], axis=1)

    heads = range(nh)
    log_gamma = [float(np.log1p(-np.exp2(np.float32(-5.0 - h)))) for h in heads]
    q = [rope(p_ref[:, h * dk:(h + 1) * dk]) for h in heads]
    k = [rope(p_ref[:, nh * dk + h * dk:nh * dk + (h + 1) * dk]) * dk ** -0.5 for h in heads]
    v = [p_ref[:, 2 * nh * dk + h * dv:2 * nh * dk + (h + 1) * dv] for h in heads]
    qk = [_mxu(qi, ki, NT_DIMS) for qi, ki in zip(q, k)]
    mem = [r_ref[h] for h in heads]
    qr = [_mxu(qi, m) for qi, m in zip(q, mem)]
    d_mat = [jnp.where(row >= col, jnp.exp(lag * lg), 0.0) for lg in log_gamma]
    intra = [_mxu(s * d, vi) for s, d, vi in zip(qk, d_mat, v)]
    upd = [_mxu(ki * jnp.exp((n - 1.0 - t_col) * lg), vi, TN_DIMS) for ki, lg, vi in zip(k, log_gamma, v)]
    for h in heads:
        o = intra[h] + qr[h] * jnp.exp((t_col + 1.0) * log_gamma[h])
        r_ref[h] = mem[h] * math.exp(n * log_gamma[h]) + upd[h]
        gate = p_ref[:, 2 * nh * dk + nh * dv + h * dv:2 * nh * dk + nh * dv + (h + 1) * dv]
        o = _rms(o, ng_ref[:, h * dv:(h + 1) * dv])
        o_ref[:, h * dv:(h + 1) * dv] = (o * _silu(gate)).astype(BF16)


def _ret_mixer(hn, positions, w_in, w_out, j, norm_g):
    s = hn.shape[0]
    n = min(RET_CHUNK, s)
    nh, dk, dv = RET_HEADS, RET_DK, RET_DV
    c_all = 2 * nh * dk + 2 * nh * dv
    proj, w_out_b = _mm_cols(hn, w_in, j, c_all, w_out, j)
    freqs = ROPE_BASE ** (-jnp.arange(0, dk, 2, dtype=F32) / dk)
    return w_out_b, pl.pallas_call(
        _ret_kernel,
        grid=(s // n,),
        in_specs=[pl.BlockSpec((n, c_all), lambda i: (i, 0)),
                  pl.BlockSpec((n, 1), lambda i: (i, 0)),
                  pl.BlockSpec((1, dk // 2), lambda i: (0, 0)),
                  pl.BlockSpec((1, nh * dv), lambda i: (0, 0))],
        out_specs=pl.BlockSpec((n, nh * dv), lambda i: (i, 0)),
        out_shape=jax.ShapeDtypeStruct((s, nh * dv), BF16),
        scratch_shapes=[pltpu.VMEM((nh, dk, dv), F32)],
        compiler_params=_params("arbitrary"),
        name="ret_chunks",
    )(proj, positions.reshape(s, 1), freqs.reshape(1, dk // 2), norm_g.reshape(1, nh * dv))


def _gelu_tanh(x):
    return 0.5 * x * (1.0 + jnp.tanh(math.sqrt(2.0 / math.pi) * (x + 0.044715 * (x * x * x))))


def _lru_kernel(p_ref, cw_ref, cb_ref, wr_ref, br_ref, wi_ref, bi_ref, lam_ref, o_ref,
                stage_ref, out_ref, halo_ref, h_ref, *, tick):
    rows = p_ref.shape[0]
    seg = rows // 8
    nb = LRU_BLOCKS
    first = pl.program_id(0) == 0

    @pl.when(first)
    def _():
        h_ref[...] = jnp.zeros_like(h_ref)
        halo_ref[...] = jnp.zeros_like(halo_ref)

    for c in range(2 * nb):
        stage_ref[c] = p_ref[:, c * LRU_BLOCK:(c + 1) * LRU_BLOCK]

    sub = lax.broadcasted_iota(jnp.int32, (8, LRU_BLOCK), 0)
    neg_c_softplus = -LRU_C * _softplus(-lam_ref[...])
    taps = CONV_WIDTH - 1

    for b in range(nb):
        lanes = slice(b * LRU_BLOCK, (b + 1) * LRU_BLOCK)
        xv = [stage_ref[b, pl.ds(v, 8, stride=seg), :] for v in range(seg)]
        wrap = [jnp.where(sub == 0,
                          pltpu.roll(halo_ref[b, taps - d], 1, axis=0),
                          pltpu.roll(xv[seg - d], 1, axis=0)) for d in range(1, taps + 1)]
        for d in range(1, taps + 1):
            halo_ref[b, taps - d] = xv[seg - d]

        def src(v):
            return xv[v] if v >= 0 else wrap[-v - 1]

        cw = cw_ref[:, lanes]
        cb = cb_ref[:, lanes]
        xblk = jnp.concatenate(
            [cw[0:1] * src(v - 3) + cw[1:2] * src(v - 2) + cw[2:3] * src(v - 1) + cw[3:4] * src(v) + cb
             for v in range(seg)], axis=0)
        r = _sigmoid(_mxu(xblk, wr_ref[b]) + br_ref[:, lanes])
        gi = _sigmoid(_mxu(xblk, wi_ref[b]) + bi_ref[:, lanes])
        log_a = r * neg_c_softplus[:, lanes]
        a = jnp.exp(log_a)
        u = jnp.sqrt(1.0 - a * a) * (gi * xblk)

        hs, ps = [u[0:8]], [a[0:8]]
        for v in range(1, seg):
            av = a[v * 8:(v + 1) * 8]
            hs.append(av * hs[-1] + u[v * 8:(v + 1) * 8])
            ps.append(av * ps[-1])
        e, pc = hs[-1], ps[-1]
        d = 1
        while d < 8:
            keep = sub >= d
            e = e + pc * jnp.where(keep, pltpu.roll(e, d, axis=0), 0.0)
            pc = pc * jnp.where(keep, pltpu.roll(pc, d, axis=0), 1.0)
            d *= 2
        h0 = h_ref[b, 0:1, :]
        end_state = e + pc * h0
        carry = jnp.where(sub == 0, h0, pltpu.roll(end_state, 1, axis=0))
        h_ref[b, 0:1, :] = end_state[7:8]
        for v in range(seg):
            gate = stage_ref[nb + b, pl.ds(v, 8, stride=seg), :]
            out_ref[b, pl.ds(v, 8, stride=seg), :] = (hs[v] + ps[v] * carry) * _gelu_tanh(gate)
        if b % 2:
            tick()

    tick.flush()
    for b in range(nb):
        o_ref[:, b * LRU_BLOCK:(b + 1) * LRU_BLOCK] = out_ref[b].astype(BF16)


def _lru_mixer(hn, w_in, w_out, j, conv_w, conv_b, w_r, b_r, w_i, b_i, lam, x, g_post, g_next):
    s = hn.shape[0]
    rows = min(LRU_ROWS, s)
    n_steps = s // rows
    w = LRU_WIDTH
    proj, w_out_b = _mm_cols(hn, w_in, j, 2 * w, w_out, j)
    vec = pl.BlockSpec((1, w), lambda i: (0, 0))
    blk = pl.BlockSpec((LRU_BLOCKS, LRU_BLOCK, LRU_BLOCK), lambda i: (0, 0, 0))
    return _chunks_with_out_proj(
        _lru_kernel, n_steps, rows,
        [pl.BlockSpec((rows, 2 * w), lambda i: (jnp.minimum(i, n_steps - 1), 0)),
         pl.BlockSpec((CONV_WIDTH, w), lambda i: (0, 0)), vec, blk, vec, blk, vec, vec],
        (proj, conv_w, conv_b.reshape(1, w), w_r, b_r.reshape(1, w), w_i, b_i.reshape(1, w), lam.reshape(1, w)),
        [pltpu.VMEM((2 * LRU_BLOCKS, rows, LRU_BLOCK), F32),
         pltpu.VMEM((LRU_BLOCKS, rows, LRU_BLOCK), F32),
         pltpu.VMEM((LRU_BLOCKS, CONV_WIDTH - 1, 8, LRU_BLOCK), F32),
         pltpu.VMEM((LRU_BLOCKS, 8, LRU_BLOCK), F32)],
        w_out_b, x, g_post, g_next, "lru_scan")


@jax.jit
def kernel(x, positions, norm_g, mlp_w_up, mlp_w_down, gdn_w_in, gdn_conv_w, gdn_a_log, gdn_dt_bias, gdn_norm_g, gdn_w_out, mlstm_w_in, mlstm_b_i, mlstm_b_f, mlstm_norm_g, mlstm_w_out, ret_w_in, ret_norm_g, ret_w_out, lru_w_in, lru_conv_w, lru_conv_b, lru_w_r, lru_b_r, lru_w_i, lru_b_i, lru_lambda, lru_w_out):
    batch, seq, _ = x.shape
    assert batch == 1
    depth = norm_g.shape[0]
    xs = x.reshape(seq, D_MODEL)
    hn = _prenorm(xs, norm_g[0, 0])
    for layer in range(depth):
        kind, j = layer % 4, layer // 4
        if kind == 0:
            w_out, mixed = _gdn_mixer(hn, gdn_w_in, gdn_w_out, j, gdn_conv_w[j], gdn_a_log[j], gdn_dt_bias[j],
                                      gdn_norm_g[j])
            xs, hn = _mm_rows(mixed, w_out, xs, norm_g[layer, 1], norm_g[layer, 2])
        elif kind == 2:
            w_out, mixed = _ret_mixer(hn, positions, ret_w_in, ret_w_out, j, ret_norm_g[j])
            xs, hn = _mm_rows(mixed, w_out, xs, norm_g[layer, 1], norm_g[layer, 2])
        elif kind == 1:
            xs, hn = _mlstm_mixer(hn, mlstm_w_in, mlstm_w_out, j, mlstm_b_i[j], mlstm_b_f[j], mlstm_norm_g[j],
                                  xs, norm_g[layer, 1], norm_g[layer, 2])
        else:
            xs, hn = _lru_mixer(hn, lru_w_in, lru_w_out, j, lru_conv_w[j], lru_conv_b[j], lru_w_r[j],
                                lru_b_r[j], lru_w_i[j], lru_b_i[j], lru_lambda[j],
                                xs, norm_g[layer, 1], norm_g[layer, 2])
        hmid, w_down = _mm_cols(hn, mlp_w_up, layer, D_FF, mlp_w_down, layer, act="relu2", out_dtype=BF16)
        g_next = norm_g[layer + 1, 0] if layer + 1 < depth else None
        xs, hn = _mm_rows(hmid, w_down, xs, norm_g[layer, 3], g_next)
    return xs.reshape(batch, seq, D_MODEL)
```
